```python
import math
import jax, jax.numpy as jnp
from jax import lax
import numpy as np

D_MODEL = 2048
BATCH = 1
SEQ = 16384
DEPTH = 4

GRID_W = 64
CTX_LEN = 256
N_MIXERS = 4
ALPHA = (2.0 * DEPTH) ** 0.25
BETA = (8.0 * DEPTH) ** -0.25
LN_EPS = 1e-5
RMS_EPS = 1e-6
ROPE_BASE = 10000.0

SWA_HEAD_DIM = 64
SWA_HEADS = D_MODEL // SWA_HEAD_DIM
SWA_KV_HEADS = SWA_HEADS // 8
SWA_WINDOW = 128
SWA_BLOCK = 128
HG_KDIM = 128
HG_HEADS = D_MODEL // HG_KDIM
HG_VDIM = D_MODEL // HG_HEADS
HG_CHUNK = 64
MLA_HEADS = 16
MLA_Q_RANK = 512
MLA_KV_RANK = 512
MLA_NOPE = 128
MLA_ROPE = 64
MLA_V = 128
MLA_QBLOCK = 128
SSM_D_INNER = 2 * D_MODEL
SSM_HEAD_DIM = 64
SSM_HEADS = SSM_D_INNER // SSM_HEAD_DIM
SSM_STATE = 128
SSM_GROUPS = 8
SSM_CONV = 3
SSM_CHUNK = 128
FFN_DIM = 5632
FFN_CONV = 3

N_SWA = len(range(0, DEPTH, N_MIXERS))
N_HG = len(range(1, DEPTH, N_MIXERS))
N_MLA = len(range(2, DEPTH, N_MIXERS))
N_SSM = len(range(3, DEPTH, N_MIXERS))

kernel_name = "hybrid_interleaved_diffusion_trunk"

F32 = jnp.float32


def _split(a, sizes):
    return jnp.split(a, np.cumsum(sizes)[:-1].tolist(), axis=-1)


def layer_norm(x, g, b):
    xf = x.astype(F32)
    mu = jnp.mean(xf, -1, keepdims=True)
    var = jnp.mean(jnp.square(xf - mu), -1, keepdims=True)
    return ((xf - mu) * lax.rsqrt(var + LN_EPS) * g + b).astype(x.dtype)


def rms_norm(x, w):
    xf = x.astype(F32)
    return (xf * lax.rsqrt(jnp.mean(xf * xf, -1, keepdims=True) + RMS_EPS) * w).astype(x.dtype)


def dwconv_centred(x, w, b):
    width = w.shape[0]
    y = lax.conv_general_dilated(
        x, w[:, None, :].astype(x.dtype), window_strides=(1,),
        padding=[(width // 2, width // 2)], dimension_numbers=('NWC', 'WIO', 'NWC'),
        feature_group_count=x.shape[-1])
    return y + b.astype(x.dtype)


def axial_rope(n_tokens, rot_dim):
    rows = n_tokens // GRID_W
    row = jnp.repeat(jnp.arange(rows, dtype=F32), GRID_W)
    col = jnp.tile(jnp.arange(GRID_W, dtype=F32), rows)
    n_axis = rot_dim // 4
    inv_freq = ROPE_BASE ** (-jnp.arange(n_axis, dtype=F32) / n_axis)
    ang = jnp.concatenate([row[:, None] * inv_freq, col[:, None] * inv_freq], -1)
    return jnp.cos(ang), jnp.sin(ang)


def apply_rope(x, cos, sin):
    half = x.shape[-1] // 2
    cos, sin = cos[None, :, None, :], sin[None, :, None, :]
    x1, x2 = x[..., :half].astype(F32), x[..., half:].astype(F32)
    return jnp.concatenate([x1 * cos - x2 * sin, x1 * sin + x2 * cos], -1).astype(x.dtype)


def _gqa_scores(q, k):
    return jnp.einsum('bqgrd,bkgd->bgrqk', q, k).astype(F32)


def _gqa_out(p, v):
    return jnp.einsum('bgrqk,bkgd->bqgrd', p.astype(v.dtype), v)


def _softmax_with_sink(s, sink):
    s_sink = jnp.broadcast_to(sink[None, :, :, None, None], s.shape[:-1] + (1,))
    return jax.nn.softmax(jnp.concatenate([s, s_sink], -1), axis=-1)[..., :-1]


def swa_mixer(h_ctx, h_lat, w_in, sink, w_out, with_ctx):
    B, T, _ = h_lat.shape
    H, G, Dh = SWA_HEADS, SWA_KV_HEADS, SWA_HEAD_DIM
    R = H // G
    scale = Dh ** -0.5
    sink = sink.astype(F32).reshape(G, R)

    def proj(h):
        n = h.shape[1]
        q, k, v = _split(h @ w_in, (H * Dh, G * Dh, G * Dh))
        return q.reshape(B, n, H, Dh), k.reshape(B, n, G, Dh), v.reshape(B, n, G, Dh)

    qc, kc, vc = proj(h_ctx)
    y_ctx = None
    if with_ctx:
        qc = qc.reshape(B, -1, G, R, Dh)
        p = _softmax_with_sink(_gqa_scores(qc, kc) * scale, sink)
        y_ctx = _gqa_out(p, vc).reshape(B, -1, H * Dh) @ w_out

    ql, kl, vl = proj(h_lat)
    cos, sin = axial_rope(T, Dh)
    ql = apply_rope(ql, cos, sin).reshape(B, T, G, R, Dh)
    kl = apply_rope(kl, cos, sin)
    BLK = SWA_BLOCK
    nb = T // BLK
    pad = ((0, 0), (BLK, BLK), (0, 0), (0, 0))
    kp, vp = jnp.pad(kl, pad), jnp.pad(vl, pad)
    q_blocks = jnp.moveaxis(ql.reshape(B, nb, BLK, G, R, Dh), 1, 0)
    offs = jnp.arange(3 * BLK) - BLK
    in_band = jnp.abs(offs[None, :] - jnp.arange(BLK)[:, None]) <= SWA_WINDOW

    def block(args):
        i, qb = args
        kw = lax.dynamic_slice_in_dim(kp, i * BLK, 3 * BLK, axis=1)
        vw = lax.dynamic_slice_in_dim(vp, i * BLK, 3 * BLK, axis=1)
        kpos = i * BLK + offs
        valid = in_band & ((kpos >= 0) & (kpos < T))[None, :]
        s_win = jnp.where(valid, _gqa_scores(qb, kw) * scale, -jnp.inf)
        s_ctx = _gqa_scores(qb, kc) * scale
        p = _softmax_with_sink(jnp.concatenate([s_win, s_ctx], -1), sink)
        return _gqa_out(p[..., :3 * BLK], vw) + _gqa_out(p[..., 3 * BLK:], vc)

    o = lax.map(block, (jnp.arange(nb), q_blocks))
    y_lat = jnp.moveaxis(o, 0, 1).reshape(B, T, H * Dh) @ w_out
    return y_ctx, y_lat


def gla_chunk_scan(q, k, v, log_f, s0, chunk=HG_CHUNK):
    Bsz, T, H, K = q.shape
    n = T // chunk
    causal = jnp.tril(jnp.ones((chunk, chunk), bool))

    def to_chunks(a):
        return jnp.moveaxis(a.astype(F32).reshape(Bsz, n, chunk, *a.shape[2:]), 1, 0)

    def step(S, inp):
        qc, kc, vc, gc = inp
        b = jnp.cumsum(gc, axis=1)
        decay = jnp.exp(jnp.where(causal[None, :, :, None, None], b[:, :, None] - b[:, None], -jnp.inf))
        att = jnp.einsum('bthk,bshk,btshk->bhts', qc, kc, decay)
        o = jnp.einsum('bhts,bshv->bthv', att, vc) + jnp.einsum('bthk,bhkv->bthv', qc * jnp.exp(b), S)
        b_last = b[:, -1]
        S = jnp.exp(b_last)[..., None] * S + jnp.einsum(
            'bshk,bshv->bhkv', kc * jnp.exp(b_last[:, None] - b), vc)
        return S, o

    S, o = lax.scan(step, s0, (to_chunks(q), to_chunks(k), to_chunks(v), to_chunks(log_f)))
    return jnp.moveaxis(o, 0, 1).reshape(Bsz, T, H, v.shape[-1]), S


def hgrn2_mixer(h_ctx, h_lat, w_in, lower_bound, norm_w, w_out, with_ctx):
    B = h_lat.shape[0]
    H, K, V = HG_HEADS, HG_KDIM, HG_VDIM
    log_lb = jnp.log(lower_bound)
    log_1mlb = jnp.log1p(-lower_bound)

    def forget(fr, n):
        log_f = jnp.logaddexp(log_lb, log_1mlb + jax.nn.log_sigmoid(fr.astype(F32))).reshape(B, n, H, K)
        return log_f, -jnp.expm1(log_f)

    def mix(h, s0_fw, s0_bw):
        n = h.shape[1]
        q, f_fw, f_bw, i, g = _split(h @ w_in, (H * K, H * K, H * K, H * V, H * V))
        q = (jax.nn.silu(q) * K ** -0.5).reshape(B, n, H, K)
        i = i.reshape(B, n, H, V)
        lf_fw, k_fw = forget(f_fw, n)
        lf_bw, k_bw = forget(f_bw, n)
        o_fw, s_fw = gla_chunk_scan(q, k_fw, i, lf_fw, s0_fw)
        fl = lambda a: jnp.flip(a, 1)
        o_bw, s_bw = gla_chunk_scan(fl(q), fl(k_bw), fl(i), fl(lf_bw), s0_bw)
        return o_fw + fl(o_bw), g, s_fw, s_bw

    def readout(o, g):
        n = o.shape[1]
        o = rms_norm(o, norm_w.reshape(H, V)).reshape(B, n, H * V)
        return (o.astype(g.dtype) * jax.nn.silu(g)) @ w_out

    zeros = jnp.zeros((B, H, K, V), F32)
    o_c, g_c, s_fw, s_bw = mix(h_ctx, zeros, zeros)
    o_l, g_l, _, _ = mix(h_lat, s_fw, s_bw)
    y_ctx = readout(o_c, g_c) if with_ctx else None
    return y_ctx, readout(o_l, g_l)


def mla_mixer(h_ctx, h_lat, w_in, q_norm, kv_norm, w_uq, w_ukv, w_out, with_ctx):
    B, T, _ = h_lat.shape
    H = MLA_HEADS
    scale = (MLA_NOPE + MLA_ROPE) ** -0.5

    def proj(h, rope):
        n = h.shape[1]
        cq, ckv, k_pe = _split(h @ w_in, (MLA_Q_RANK, MLA_KV_RANK, MLA_ROPE))
        q = (rms_norm(cq, q_norm) @ w_uq).reshape(B, n, H, MLA_NOPE + MLA_ROPE)
        kv = (rms_norm(ckv, kv_norm) @ w_ukv).reshape(B, n, H, MLA_NOPE + MLA_V)
        q_nope, q_pe = q[..., :MLA_NOPE], q[..., MLA_NOPE:]
        k_nope, v = kv[..., :MLA_NOPE], kv[..., MLA_NOPE:]
        k_pe = k_pe[:, :, None, :]
        if rope is not None:
            q_pe = apply_rope(q_pe, *rope)
            k_pe = apply_rope(k_pe, *rope)
        return q_nope, q_pe, k_nope, k_pe[:, :, 0], v

    def scores(qn, qp, kn, kp):
        return (jnp.einsum('bqhd,bkhd->bhqk', qn, kn)
                + jnp.einsum('bqhd,bkd->bhqk', qp, kp)).astype(F32) * scale

    def out(p, v):
        return jnp.einsum('bhqk,bkhd->bqhd', p.astype(v.dtype), v)

    cqn, cqp, ckn, ckp, cv = proj(h_ctx, None)
    y_ctx = None
    if with_ctx:
        p = jax.nn.softmax(scores(cqn, cqp, ckn, ckp), axis=-1)
        y_ctx = out(p, cv).reshape(B, -1, H * MLA_V) @ w_out

    lqn, lqp, lkn, lkp, lv = proj(h_lat, axial_rope(T, MLA_ROPE))
    nb = T // MLA_QBLOCK

    def to_blocks(a):
        return jnp.moveaxis(a.reshape(B, nb, MLA_QBLOCK, *a.shape[2:]), 1, 0)

    def block(qs):
        qn, qp = qs
        s = jnp.concatenate([scores(qn, qp, lkn, lkp), scores(qn, qp, ckn, ckp)], -1)
        p = jax.nn.softmax(s, axis=-1)
        return out(p[..., :T], lv) + out(p[..., T:], cv)

    o = lax.map(block, (to_blocks(lqn), to_blocks(lqp)))
    y_lat = jnp.moveaxis(o, 0, 1).reshape(B, T, H * MLA_V) @ w_out
    return y_ctx, y_lat


def ssd_chunk_scan(x, dt, A, Bm, Cm, s0, chunk=SSM_CHUNK):
    Bsz, T, H, P = x.shape
    G, N = Bm.shape[2], Bm.shape[3]
    R = H // G
    n = T // chunk
    causal = jnp.tril(jnp.ones((chunk, chunk), bool))

    def to_chunks(a):
        return jnp.moveaxis(a.astype(F32).reshape(Bsz, n, chunk, *a.shape[2:]), 1, 0)

    def step(S, inp):
        xc, dtc, Bc, Cc = inp
        a = jnp.cumsum(dtc * A, axis=1)
        seg = a[:, :, None, :] - a[:, None, :, :]
        lmat = jnp.exp(jnp.where(causal[None, :, :, None], seg, -jnp.inf))
        w = (jnp.einsum('btgn,bsgn->btsg', Cc, Bc)[..., None]
             * lmat.reshape(Bsz, chunk, chunk, G, R) * dtc.reshape(Bsz, 1, chunk, G, R))
        xg = xc.reshape(Bsz, chunk, G, R, P)
        Sg = S.reshape(Bsz, G, R, N, P)
        y = (jnp.einsum('btsgr,bsgrp->btgrp', w, xg)
             + jnp.einsum('btgn,bgrnp->btgrp', Cc, Sg) * jnp.exp(a).reshape(Bsz, chunk, G, R, 1))
        a_last = a[:, -1]
        w_state = (jnp.exp(a_last[:, None] - a) * dtc).reshape(Bsz, chunk, G, R)
        Sg = (jnp.exp(a_last).reshape(Bsz, G, R, 1, 1) * Sg
              + jnp.einsum('bsgn,bsgr,bsgrp->bgrnp', Bc, w_state, xg))
        return Sg.reshape(Bsz, H, N, P), y.reshape(Bsz, chunk, H, P)

    S, y = lax.scan(step, s0, (to_chunks(x), to_chunks(dt), to_chunks(Bm), to_chunks(Cm)))
    return jnp.moveaxis(y, 0, 1).reshape(Bsz, T, H, P), S


def mamba2_mixer(h_ctx, h_lat, w_in, conv_w, conv_b, dt_bias, A_log, D_skip, norm_w, w_out, with_ctx):
    B = h_lat.shape[0]
    H, P, G, N, DI = SSM_HEADS, SSM_HEAD_DIM, SSM_GROUPS, SSM_STATE, SSM_D_INNER
    A = -jnp.exp(A_log.astype(F32))
    d_skip = D_skip.astype(F32)[:, None]

    def mix(h, s0_fw, s0_bw):
        n = h.shape[1]
        z, xbc, dt = _split(h @ w_in, (DI, DI + 2 * G * N, 2 * H))
        xbc = jax.nn.silu(dwconv_centred(xbc, conv_w, conv_b))
        xs, Bm, Cm = _split(xbc, (DI, G * N, G * N))
        xs = xs.reshape(B, n, H, P)
        Bm, Cm = Bm.reshape(B, n, G, N), Cm.reshape(B, n, G, N)
        dt = jax.nn.softplus(dt.astype(F32).reshape(B, n, 2, H) + dt_bias.astype(F32))
        y_fw, s_fw = ssd_chunk_scan(xs, dt[:, :, 0], A[0], Bm, Cm, s0_fw)
        fl = lambda a: jnp.flip(a, 1)
        y_bw, s_bw = ssd_chunk_scan(fl(xs), fl(dt[:, :, 1]), A[1], fl(Bm), fl(Cm), s0_bw)
        y = y_fw + fl(y_bw) + d_skip * xs.astype(F32)
        return y, z, s_fw, s_bw

    def readout(y, z):
        n = y.shape[1]
        y = y.reshape(B, n, DI) * jax.nn.silu(z.astype(F32))
        y = rms_norm(y.reshape(B, n, G, DI // G), norm_w.reshape(G, DI // G)).reshape(B, n, DI)
        return y.astype(z.dtype) @ w_out

    zeros = jnp.zeros((B, H, N, P), F32)
    y_c, z_c, s_fw, s_bw = mix(h_ctx, zeros, zeros)
    y_l, z_l, _, _ = mix(h_lat, s_fw, s_bw)
    y_ctx = readout(y_c, z_c) if with_ctx else None
    return y_ctx, readout(y_l, z_l)


def conv_ffn(h, w_up, conv_w, conv_b, w_down):
    a, v = jnp.split(h @ w_up, 2, axis=-1)
    return (jax.nn.silu(dwconv_centred(a, conv_w, conv_b)) * v) @ w_down


def setup_inputs(seed: int = 0) -> dict:
    key = jax.random.key(seed)
    keys = iter(jax.random.split(key, 40))

    def normal(shape, std):
        return std * jax.random.normal(next(keys), shape, jnp.float32)

    def linear(shape, gain=1.0):
        return normal(shape, gain * shape[-2] ** -0.5)

    def norm_gain(shape):
        return 1.0 + normal(shape, 0.02)

    D, F = D_MODEL, FFN_DIM
    swa_cols = (SWA_HEADS + 2 * SWA_KV_HEADS) * SWA_HEAD_DIM
    hg_cols = 3 * HG_HEADS * HG_KDIM + 2 * HG_HEADS * HG_VDIM
    mla_cols = MLA_Q_RANK + MLA_KV_RANK + MLA_ROPE
    ssm_conv_ch = SSM_D_INNER + 2 * SSM_GROUPS * SSM_STATE
    ssm_cols = SSM_D_INNER + ssm_conv_ch + 2 * SSM_HEADS
    dt0 = jnp.exp(jax.random.uniform(next(keys), (N_SSM, 2, SSM_HEADS), jnp.float32,
                                     math.log(1e-3), math.log(1e-1)))
    return {
        "x": normal((BATCH, SEQ, D), 1.0),
        "c": normal((BATCH, D), 1.0),
        "ctx": normal((BATCH, CTX_LEN, D), 1.0),
        "c_ctx": normal((D,), 1.0),
        "ada_w": linear((DEPTH, D, 6 * D)),
        "ada_b": normal((DEPTH, 6 * D), 0.02),
        "ln_g": norm_gain((DEPTH, 2, D)),
        "ln_b": normal((DEPTH, 2, D), 0.02),
        "ffn_up": linear((DEPTH, D, 2 * F)),
        "ffn_conv_w": normal((DEPTH, FFN_CONV, F), FFN_CONV ** -0.5),
        "ffn_conv_b": normal((DEPTH, F), 0.02),
        "ffn_down": linear((DEPTH, F, D), BETA),
        "swa_in": linear((N_SWA, D, swa_cols)),
        "swa_sink": normal((N_SWA, SWA_HEADS), 1.0),
        "swa_out": linear((N_SWA, SWA_HEADS * SWA_HEAD_DIM, D), BETA),
        "hg_in": linear((N_HG, D, hg_cols)),
        "hg_lb_logits": normal((DEPTH, HG_HEADS * HG_KDIM), 0.5),
        "hg_norm_w": norm_gain((N_HG, HG_HEADS * HG_VDIM)),
        "hg_out": linear((N_HG, HG_HEADS * HG_VDIM, D), BETA),
        "mla_in": linear((N_MLA, D, mla_cols)),
        "mla_q_norm": norm_gain((N_MLA, MLA_Q_RANK)),
        "mla_kv_norm": norm_gain((N_MLA, MLA_KV_RANK)),
        "mla_uq": linear((N_MLA, MLA_Q_RANK, MLA_HEADS * (MLA_NOPE + MLA_ROPE))),
        "mla_ukv": linear((N_MLA, MLA_KV_RANK, MLA_HEADS * (MLA_NOPE + MLA_V))),
        "mla_out": linear((N_MLA, MLA_HEADS * MLA_V, D), BETA),
        "ssm_in": linear((N_SSM, D, ssm_cols)),
        "ssm_conv_w": normal((N_SSM, SSM_CONV, ssm_conv_ch), SSM_CONV ** -0.5),
        "ssm_conv_b": normal((N_SSM, ssm_conv_ch), 0.02),
        "ssm_dt_bias": dt0 + jnp.log(-jnp.expm1(-dt0)),
        "ssm_A_log": jnp.log(jax.random.uniform(next(keys), (N_SSM, 2, SSM_HEADS), jnp.float32, 1.0, 16.0)),
        "ssm_D": norm_gain((N_SSM, SSM_HEADS)),
        "ssm_norm_w": norm_gain((N_SSM, SSM_D_INNER)),
        "ssm_out": linear((N_SSM, SSM_D_INNER, D), BETA),
    }


def reference(x, c, ctx, c_ctx, ada_w, ada_b, ln_g, ln_b, ffn_up, ffn_conv_w, ffn_conv_b, ffn_down,
              swa_in, swa_sink, swa_out, hg_in, hg_lb_logits, hg_norm_w, hg_out,
              mla_in, mla_q_norm, mla_kv_norm, mla_uq, mla_ukv, mla_out,
              ssm_in, ssm_conv_w, ssm_conv_b, ssm_dt_bias, ssm_A_log, ssm_D, ssm_norm_w, ssm_out):
    probs = jax.nn.softmax(hg_lb_logits.astype(F32), axis=0)
    lower_bounds = jnp.cumsum(probs, axis=0) - probs[0]

    x_lat, x_ctx = x, ctx
    for i in range(DEPTH):
        kind, j = i % N_MIXERS, i // N_MIXERS
        with_ctx = i < DEPTH - 1
        mod_l = jnp.split((jax.nn.silu(c) @ ada_w[i] + ada_b[i])[:, None, :], 6, axis=-1)
        mod_c = jnp.split(jax.nn.silu(c_ctx) @ ada_w[i] + ada_b[i], 6, axis=-1)
        h_lat = x_lat * (1 + mod_l[1]) + mod_l[0]
        h_ctx = x_ctx * (1 + mod_c[1]) + mod_c[0]
        if kind == 0:
            y_ctx, y_lat = swa_mixer(h_ctx, h_lat, swa_in[j], swa_sink[j], swa_out[j], with_ctx)
        elif kind == 1:
            y_ctx, y_lat = hgrn2_mixer(h_ctx, h_lat, hg_in[j], lower_bounds[i], hg_norm_w[j], hg_out[j],
                                       with_ctx)
        elif kind == 2:
            y_ctx, y_lat = mla_mixer(h_ctx, h_lat, mla_in[j], mla_q_norm[j], mla_kv_norm[j], mla_uq[j],
                                     mla_ukv[j], mla_out[j], with_ctx)
        else:
            y_ctx, y_lat = mamba2_mixer(h_ctx, h_lat, ssm_in[j], ssm_conv_w[j], ssm_conv_b[j], ssm_dt_bias[j],
                                        ssm_A_log[j], ssm_D[j], ssm_norm_w[j], ssm_out[j], with_ctx)
        x_lat = layer_norm(ALPHA * x_lat + mod_l[2] * y_lat, ln_g[i, 0], ln_b[i, 0])
        f_lat = conv_ffn(x_lat * (1 + mod_l[4]) + mod_l[3], ffn_up[i], ffn_conv_w[i], ffn_conv_b[i], ffn_down[i])
        x_lat = layer_norm(ALPHA * x_lat + mod_l[5] * f_lat, ln_g[i, 1], ln_b[i, 1])
        if with_ctx:
            x_ctx = layer_norm(ALPHA * x_ctx + mod_c[2] * y_ctx, ln_g[i, 0], ln_b[i, 0])
            f_ctx = conv_ffn(x_ctx * (1 + mod_c[4]) + mod_c[3], ffn_up[i], ffn_conv_w[i], ffn_conv_b[i],
                             ffn_down[i])
            x_ctx = layer_norm(ALPHA * x_ctx + mod_c[5] * f_ctx, ln_g[i, 1], ln_b[i, 1])
    return x_lat
```

```python
import functools

import jax
import jax.numpy as jnp
from jax import lax
from jax.experimental import pallas as pl
from jax.experimental.pallas import tpu as pltpu

F32, BF16 = jnp.float32, jnp.bfloat16

DEPTH, N_MIXERS = 4, 4
GRID_W = 64
ALPHA = (2.0 * DEPTH) ** 0.25
LN_EPS, RMS_EPS, ROPE_BASE = 1e-5, 1e-6, 10000.0
SWA_HEADS, SWA_KV_HEADS, SWA_HEAD_DIM, SWA_BLOCK = 32, 4, 64, 128
HG_HEADS, HG_KDIM = 16, 128
MLA_HEADS, MLA_Q_RANK, MLA_KV_RANK, MLA_NOPE, MLA_ROPE, MLA_V = 16, 512, 512, 128, 64, 128
SSM_HEAD_DIM, SSM_STATE, SSM_GROUPS, SSM_CHUNK = 64, 128, 8, 128
HG_CHUNK = 64
LANES = 128
NEG = -1e30
VMEM_LIMIT = 56 * 1024 * 1024


def _cp(*sem):
    return pltpu.CompilerParams(dimension_semantics=sem, vmem_limit_bytes=VMEM_LIMIT)


def _dot(a, b):
    return jnp.dot(a, b, preferred_element_type=F32)


def _dot_nt(a, b):
    return lax.dot_general(a, b, (((1,), (1,)), ((), ())), preferred_element_type=F32)


def _dot_tn(a, b):
    return lax.dot_general(a, b, (((0,), (0,)), ((), ())), preferred_element_type=F32)


def _sigmoid(x):
    return 1.0 / (1.0 + jnp.exp(-x))


def _silu(x):
    return x * _sigmoid(x)


def _softplus(x):
    return jnp.maximum(x, 0.0) + jnp.log(1.0 + jnp.exp(-jnp.abs(x)))


def _ln_mod(z, g, b):
    mu = jnp.mean(z, -1, keepdims=True)
    d = z - mu
    var = jnp.mean(d * d, -1, keepdims=True)
    return d * lax.rsqrt(var + LN_EPS) * g + b


def _shift_rows(x, prev_row, next_row):
    n = x.shape[0]
    row = lax.broadcasted_iota(jnp.int32, (n, 1), 0)
    dn = jnp.where(row == 0, prev_row, pltpu.roll(x, 1, 0))
    up = jnp.where(row == n - 1, next_row, pltpu.roll(x, n - 1, 0))
    return dn, up


def _cumsum_rows(x, reverse):
    n = x.shape[0]
    row = lax.broadcasted_iota(jnp.int32, (n, 1), 0)
    s = 1
    while s < n:
        if reverse:
            x = x + jnp.where(row < n - s, pltpu.roll(x, n - s, 0), 0.0)
        else:
            x = x + jnp.where(row >= s, pltpu.roll(x, s, 0), 0.0)
        s *= 2
    return x


def _ada_kernel(c_ref, w_ref, b_ref, o_ref):
    a = _silu(c_ref[...]).astype(BF16)
    o_ref[0] = _dot(a, w_ref[0].astype(BF16)) + b_ref[0]


def ada_modulation(cvec, ada_w, ada_b):
    depth, d, n = ada_w.shape
    tn = 1024
    return pl.pallas_call(
        _ada_kernel,
        out_shape=jax.ShapeDtypeStruct((depth, 8, n), F32),
        grid=(depth, n // tn),
        in_specs=[pl.BlockSpec((8, d), lambda i, j: (0, 0)),
                  pl.BlockSpec((1, d, tn), lambda i, j: (i, 0, j)),
                  pl.BlockSpec((1, 1, tn), lambda i, j: (i, 0, j))],
        out_specs=pl.BlockSpec((1, 8, tn), lambda i, j: (i, 0, j)),
        compiler_params=_cp("parallel", "parallel"),
        name="ada_modulation",
    )(cvec, ada_w, ada_b.reshape(depth, 1, n))


def _modulate_kernel(x_ref, sh_ref, sc_ref, o_ref):
    o_ref[...] = (x_ref[...] * (1.0 + sc_ref[...]) + sh_ref[...]).astype(o_ref.dtype)


def modulate(x, shift, scale):
    m, d = x.shape
    tm = min(512, m)
    vec = pl.BlockSpec((1, d), lambda i: (0, 0))
    return pl.pallas_call(
        _modulate_kernel,
        out_shape=jax.ShapeDtypeStruct((m, d), BF16),
        grid=(m // tm,),
        in_specs=[pl.BlockSpec((tm, d), lambda i: (i, 0)), vec, vec],
        out_specs=pl.BlockSpec((tm, d), lambda i: (i, 0)),
        compiler_params=_cp("parallel"),
        name="modulate",
    )(x, shift, scale)


def _mm_kernel(a_ref, w_ref, o_ref):
    o_ref[...] = _dot(a_ref[...], w_ref[...]).astype(o_ref.dtype)


def matmul(a, w, out_dtype, tn, tm=512):
    m, k = a.shape
    n = w.shape[1]
    tm = min(tm, m)
    assert m % tm == 0 and n % tn == 0
    return pl.pallas_call(
        _mm_kernel,
        out_shape=jax.ShapeDtypeStruct((m, n), out_dtype),
        grid=(m // tm, n // tn),
        in_specs=[pl.BlockSpec((tm, k), lambda i, j: (i, 0)),
                  pl.BlockSpec((k, tn), lambda i, j: (0, j))],
        out_specs=pl.BlockSpec((tm, tn), lambda i, j: (i, j)),
        compiler_params=_cp("parallel", "parallel"),
        name="matmul",
    )(a, w)


def _mm_ln_kernel(a_ref, w_ref, x_ref, gate_ref, g_ref, b_ref, sh_ref, sc_ref, xo_ref, ho_ref, acc_ref):
    k = pl.program_id(1)

    @pl.when(k == 0)
    def _():
        acc_ref[...] = jnp.zeros_like(acc_ref)

    acc_ref[...] += _dot(a_ref[...], w_ref[...])

    @pl.when(k == pl.num_programs(1) - 1)
    def _():
        xn = _ln_mod(ALPHA * x_ref[...] + gate_ref[...] * acc_ref[...], g_ref[...], b_ref[...])
        xo_ref[...] = xn
        ho_ref[...] = (xn * (1.0 + sc_ref[...]) + sh_ref[...]).astype(ho_ref.dtype)


def matmul_ln(a, w, x, gate, g, b, shift, scale):
    m, k = a.shape
    d = w.shape[1]
    tm, tk = min(256, m), 2048
    vec = pl.BlockSpec((1, d), lambda i, kk: (0, 0))
    row = pl.BlockSpec((tm, d), lambda i, kk: (i, 0))
    return pl.pallas_call(
        _mm_ln_kernel,
        out_shape=(jax.ShapeDtypeStruct((m, d), F32), jax.ShapeDtypeStruct((m, d), BF16)),
        grid=(m // tm, k // tk),
        in_specs=[pl.BlockSpec((tm, tk), lambda i, kk: (i, kk)),
                  pl.BlockSpec((tk, d), lambda i, kk: (kk, 0)),
                  row, vec, vec, vec, vec, vec],
        out_specs=(row, row),
        scratch_shapes=[pltpu.VMEM((tm, d), F32)],
        compiler_params=_cp("parallel", "arbitrary"),
        name="matmul_ln",
    )(a, w, x, gate, g, b, shift, scale)


def _ffn_kernel(h_ref, hp_ref, hn_ref, wa_ref, wv_ref, cw_ref, cb_ref, wd_ref, x_ref, gate_ref, g_ref, b_ref,
                sh_ref, sc_ref, xo_ref, ho_ref, acc_ref):
    i, f = pl.program_id(0), pl.program_id(1)

    @pl.when(f == 0)
    def _():
        acc_ref[...] = jnp.zeros_like(acc_ref)

    wa = wa_ref[...]
    a = _dot(h_ref[...], wa)
    v = _dot(h_ref[...], wv_ref[...])
    nrow = hp_ref.shape[0]
    a_prev = _dot(hp_ref[...], wa)[nrow - 1:nrow, :]
    a_next = _dot(hn_ref[...], wa)[0:1, :]
    a_prev = jnp.where(i > 0, a_prev, 0.0)
    a_next = jnp.where(i < pl.num_programs(0) - 1, a_next, 0.0)
    dn, up = _shift_rows(a, a_prev, a_next)
    cw = cw_ref[...]
    cv = cw[0:1, :] * dn + cw[1:2, :] * a + cw[2:3, :] * up + cb_ref[...]
    u = (_silu(cv) * v).astype(BF16)
    acc_ref[...] += _dot(u, wd_ref[...])

    @pl.when(f == pl.num_programs(1) - 1)
    def _():
        xn = _ln_mod(ALPHA * x_ref[...] + gate_ref[...] * acc_ref[...], g_ref[...], b_ref[...])
        xo_ref[...] = xn
        ho_ref[...] = (xn * (1.0 + sc_ref[...]) + sh_ref[...]).astype(ho_ref.dtype)


def conv_ffn_ln(h, w_up, conv_w, conv_b, w_down, x, gate, g, b, shift, scale):
    m, d = h.shape
    f = w_down.shape[0]
    tm, tf, halo = min(512, m), 512, 16
    nf = f // tf
    nh = m // halo
    vec = pl.BlockSpec((1, d), lambda i, j: (0, 0))
    row = pl.BlockSpec((tm, d), lambda i, j: (i, 0))
    return pl.pallas_call(
        _ffn_kernel,
        out_shape=(jax.ShapeDtypeStruct((m, d), F32), jax.ShapeDtypeStruct((m, d), BF16)),
        grid=(m // tm, nf),
        in_specs=[row,
                  pl.BlockSpec((halo, d), lambda i, j: (jnp.maximum(i * (tm // halo) - 1, 0), 0)),
                  pl.BlockSpec((halo, d), lambda i, j: (jnp.minimum((i + 1) * (tm // halo), nh - 1), 0)),
                  pl.BlockSpec((d, tf), lambda i, j: (0, j)),
                  pl.BlockSpec((d, tf), lambda i, j: (0, nf + j)),
                  pl.BlockSpec((3, tf), lambda i, j: (0, j)),
                  pl.BlockSpec((1, tf), lambda i, j: (0, j)),
                  pl.BlockSpec((tf, d), lambda i, j: (j, 0)),
                  row, vec, vec, vec, vec, vec],
        out_specs=(row, row),
        scratch_shapes=[pltpu.VMEM((tm, d), F32)],
        compiler_params=_cp("parallel", "arbitrary"),
        name="conv_ffn_ln",
    )(h, h, h, w_up, w_up, conv_w, conv_b.reshape(1, f), w_down, x, gate, g, b, shift, scale)


def rope_tables(n_tokens):
    rows = n_tokens // GRID_W
    row = jnp.repeat(jnp.arange(rows, dtype=F32), GRID_W)
    col = jnp.tile(jnp.arange(GRID_W, dtype=F32), rows)
    n_axis = 16
    inv_freq = ROPE_BASE ** (-jnp.arange(n_axis, dtype=F32) / n_axis)
    ang = jnp.concatenate([row[:, None] * inv_freq, col[:, None] * inv_freq], -1)
    cos, sin = jnp.cos(ang), jnp.sin(ang)
    return jnp.concatenate([cos, cos], -1), jnp.concatenate([-sin, sin], -1)


def _rope2(x, cos, sin):
    lane = lax.broadcasted_iota(jnp.int32, x.shape, 1)
    rot = jnp.where((lane & 63) < 32, pltpu.roll(x, 96, 1), pltpu.roll(x, 32, 1))
    return x * cos + rot * sin


def _dup_half(x, upper):
    lane = lax.broadcasted_iota(jnp.int32, x.shape, 1)
    keep = (lane >= 64) if upper else (lane < 64)
    return jnp.where(keep, x, pltpu.roll(x, 64, 1))


def _swa_pairs(sink_ref, q_ref, keys, vals, bias, o_ref, rope):
    blk = q_ref.shape[0]
    lane = lax.broadcasted_iota(jnp.int32, (blk, LANES), 1)
    row2 = lax.broadcasted_iota(jnp.int32, (2 * blk, 1), 0)
    pairs_per_group = SWA_HEADS // SWA_KV_HEADS // 2
    for hp in range(SWA_HEADS // 2):
        g = hp // pairs_per_group
        q2 = q_ref[:, hp * LANES:(hp + 1) * LANES]
        if rope is not None:
            q2 = _rope2(q2, *rope)
        q2 = q2 * SWA_HEAD_DIM ** -0.5
        qq = jnp.concatenate([jnp.where(lane < 64, q2, 0.0), jnp.where(lane >= 64, q2, 0.0)], 0).astype(BF16)
        s = _dot_nt(qq, keys[g])
        if bias is not None:
            s = s + bias
        sink = jnp.where(row2 < blk, sink_ref[2 * hp], sink_ref[2 * hp + 1])
        m = jnp.maximum(jnp.max(s, -1, keepdims=True), sink)
        p = jnp.exp(s - m)
        l = jnp.sum(p, -1, keepdims=True) + jnp.exp(sink - m)
        o2 = _dot(p.astype(BF16), vals[g]) / l
        o_ref[:, hp * LANES:(hp + 1) * LANES] = jnp.where(lane < 64, o2[:blk], o2[blk:]).astype(o_ref.dtype)


def _swa_lat_kernel(sink_ref, q_ref, kp_ref, kc_ref, kn_ref, vp_ref, vc_ref, vn_ref, kx_ref, vx_ref,
                    cq_ref, sq_ref, cp_ref, sp_ref, cn_ref, sn_ref, o_ref):
    i, nb = pl.program_id(0), pl.num_programs(0)
    blk = q_ref.shape[0]
    nctx = kx_ref.shape[0]
    keys, vals = [], []
    for g in range(SWA_KV_HEADS):
        sl = slice((g // 2) * LANES, (g // 2 + 1) * LANES)
        up = g % 2 == 1
        kcat = jnp.concatenate([
            _dup_half(_rope2(kp_ref[:, sl], cp_ref[...], sp_ref[...]), up),
            _dup_half(_rope2(kc_ref[:, sl], cq_ref[...], sq_ref[...]), up),
            _dup_half(_rope2(kn_ref[:, sl], cn_ref[...], sn_ref[...]), up),
            _dup_half(kx_ref[:, sl], up)], 0)
        vcat = jnp.concatenate([_dup_half(vp_ref[:, sl], up), _dup_half(vc_ref[:, sl], up),
                                _dup_half(vn_ref[:, sl], up), _dup_half(vx_ref[:, sl], up)], 0)
        keys.append(kcat.astype(BF16))
        vals.append(vcat.astype(BF16))
    nk = 3 * blk + nctx
    r = lax.broadcasted_iota(jnp.int32, (2 * blk, nk), 0) & (blk - 1)
    c = lax.broadcasted_iota(jnp.int32, (2 * blk, nk), 1)
    has_prev = jnp.where(i > 0, 0.0, NEG)
    has_next = jnp.where(i < nb - 1, 0.0, NEG)
    bias = jnp.where(c < blk, jnp.where(c >= r, has_prev, NEG),
                     jnp.where(c < 2 * blk, 0.0,
                               jnp.where(c < 3 * blk, jnp.where(c - 2 * blk <= r, has_next, NEG), 0.0)))
    _swa_pairs(sink_ref, q_ref, keys, vals, bias, o_ref, (cq_ref[...], sq_ref[...]))


def _swa_ctx_kernel(sink_ref, q_ref, kx_ref, vx_ref, o_ref):
    keys, vals = [], []
    for g in range(SWA_KV_HEADS):
        sl = slice((g // 2) * LANES, (g // 2 + 1) * LANES)
        keys.append(_dup_half(kx_ref[:, sl], g % 2 == 1).astype(BF16))
        vals.append(_dup_half(vx_ref[:, sl], g % 2 == 1).astype(BF16))
    _swa_pairs(sink_ref, q_ref, keys, vals, None, o_ref, None)


def swa_attention(qkv_lat, qkv_ctx, sink, cos2, sin2, with_ctx):
    t, nctx = qkv_lat.shape[0], qkv_ctx.shape[0]
    blk, dq = SWA_BLOCK, SWA_HEADS * SWA_HEAD_DIM
    dkv = SWA_KV_HEADS * SWA_HEAD_DIM
    kcol, vcol = dq // dkv, dq // dkv + 1
    nb = t // blk
    cos = jnp.concatenate([cos2, cos2], -1)
    sin = jnp.concatenate([sin2, sin2], -1)
    smem = pl.BlockSpec(memory_space=pltpu.SMEM)
    prev = lambda i: jnp.maximum(i - 1, 0)
    nxt = lambda i: jnp.minimum(i + 1, nb - 1)
    tab = lambda f: pl.BlockSpec((blk, LANES), lambda i: (f(i), 0))
    kv = lambda f, col: pl.BlockSpec((blk, dkv), lambda i: (f(i), col))
    same = lambda i: i
    o_lat = pl.pallas_call(
        _swa_lat_kernel,
        out_shape=jax.ShapeDtypeStruct((t, dq), BF16),
        grid=(nb,),
        in_specs=[smem, pl.BlockSpec((blk, dq), lambda i: (i, 0)),
                  kv(prev, kcol), kv(same, kcol), kv(nxt, kcol), kv(prev, vcol), kv(same, vcol), kv(nxt, vcol),
                  pl.BlockSpec((nctx, dkv), lambda i: (0, kcol)), pl.BlockSpec((nctx, dkv), lambda i: (0, vcol)),
                  tab(same), tab(same), tab(prev), tab(prev), tab(nxt), tab(nxt)],
        out_specs=pl.BlockSpec((blk, dq), lambda i: (i, 0)),
        compiler_params=_cp("parallel"),
        name="swa_lat",
    )(sink, qkv_lat, qkv_lat, qkv_lat, qkv_lat, qkv_lat, qkv_lat, qkv_lat, qkv_ctx, qkv_ctx,
      cos, sin, cos, sin, cos, sin)
    if not with_ctx:
        return o_lat, None
    o_ctx = pl.pallas_call(
        _swa_ctx_kernel,
        out_shape=jax.ShapeDtypeStruct((nctx, dq), BF16),
        grid=(nctx // blk,),
        in_specs=[smem, pl.BlockSpec((blk, dq), lambda i: (i, 0)),
                  pl.BlockSpec((nctx, dkv), lambda i: (0, kcol)), pl.BlockSpec((nctx, dkv), lambda i: (0, vcol))],
        out_specs=pl.BlockSpec((blk, dq), lambda i: (i, 0)),
        compiler_params=_cp("parallel"),
        name="swa_ctx",
    )(sink, qkv_ctx, qkv_ctx, qkv_ctx)
    return o_lat, o_ctx


def _hg_kernel(*refs, reverse, final):
    if final:
        q_ref, f_ref, v_ref, lb_ref, s0_ref, g_ref, ofw_ref, nw_ref, o_ref, sfin_ref, st_ref = refs
    else:
        q_ref, f_ref, v_ref, lb_ref, s0_ref, o_ref, sfin_ref, st_ref = refs
    c = pl.program_id(0)

    @pl.when(c == 0)
    def _():
        st_ref[...] = s0_ref[...]

    n = q_ref.shape[0]
    kd = HG_KDIM
    lb = lb_ref[...]
    q = _silu(q_ref[...]) * kd ** -0.5
    sg = _sigmoid(f_ref[...])
    lf = jnp.log(lb + (1.0 - lb) * sg)
    k = (1.0 - lb) * (1.0 - sg)
    v = v_ref[...].astype(BF16)
    b = _cumsum_rows(lf, reverse)
    mid = n // 2 if reverse else n // 2 - 1
    last = 0 if reverse else n - 1
    r = b[mid:mid + 1, :]
    b_last = b[last:last + 1, :]
    qe = (q * jnp.exp(b - r)).astype(BF16)
    ke = (k * jnp.exp(r - b)).astype(BF16)
    qs = (q * jnp.exp(b)).astype(BF16)
    kl = (k * jnp.exp(b_last - b)).astype(BF16)
    dec = jnp.exp(b_last)
    ti = lax.broadcasted_iota(jnp.int32, (n, n), 0)
    si = lax.broadcasted_iota(jnp.int32, (n, n), 1)
    mask = (si >= ti) if reverse else (si <= ti)
    for h in range(q_ref.shape[1] // kd):
        sl = slice(h * kd, (h + 1) * kd)
        att = jnp.where(mask, _dot_nt(qe[:, sl], ke[:, sl]), 0.0).astype(BF16)
        st = st_ref[h]
        o = _dot(att, v[:, sl]) + _dot_nt(qs[:, sl], st.astype(BF16))
        st_ref[h] = st * dec[:, sl] + _dot_tn(v[:, sl], kl[:, sl])
        if final:
            o = o + ofw_ref[:, sl]
            o = o * lax.rsqrt(jnp.mean(o * o, -1, keepdims=True) + RMS_EPS) * nw_ref[:, sl]
            o_ref[:, sl] = (o * _silu(g_ref[:, sl])).astype(o_ref.dtype)
        else:
            o_ref[:, sl] = o

    @pl.when(c == pl.num_programs(0) - 1)
    def _():
        sfin_ref[...] = st_ref[...]


def hgrn2_scan(proj, lb, s0, reverse, o_fw=None, norm_w=None):
    n = proj.shape[0]
    hk = HG_HEADS * HG_KDIM
    lc = HG_CHUNK
    nc = n // lc
    final = o_fw is not None
    cidx = (lambda c: nc - 1 - c) if reverse else (lambda c: c)
    blk = lambda col: pl.BlockSpec((lc, hk), lambda c: (cidx(c), col))
    vec = pl.BlockSpec((1, hk), lambda c: (0, 0))
    st = pl.BlockSpec((HG_HEADS, HG_KDIM, HG_KDIM), lambda c: (0, 0, 0))
    in_specs = [blk(0), blk(2 if reverse else 1), blk(3), vec, st]
    args = [proj, proj, proj, lb, s0]
    if final:
        in_specs += [blk(4), blk(0), vec]
        args += [proj, o_fw, norm_w]
    return pl.pallas_call(
        functools.partial(_hg_kernel, reverse=reverse, final=final),
        out_shape=(jax.ShapeDtypeStruct((n, hk), BF16 if final else F32),
                   jax.ShapeDtypeStruct((HG_HEADS, HG_KDIM, HG_KDIM), F32)),
        grid=(nc,),
        in_specs=in_specs,
        out_specs=(blk(0), st),
        scratch_shapes=[pltpu.VMEM((HG_HEADS, HG_KDIM, HG_KDIM), F32)],
        compiler_params=_cp("arbitrary"),
        name="hgrn2_bw" if reverse else "hgrn2_fw",
    )(*args)


def _mla_norm_kernel(p_ref, qn_ref, kvn_ref, cos_ref, sin_ref, cq_ref, ckv_ref, kpe_ref):
    cq = p_ref[:, :MLA_Q_RANK]
    ckv = p_ref[:, MLA_Q_RANK:MLA_Q_RANK + MLA_KV_RANK]
    kp = p_ref[:, MLA_Q_RANK + MLA_KV_RANK:]
    cq_ref[...] = (cq * lax.rsqrt(jnp.mean(cq * cq, -1, keepdims=True) + RMS_EPS) * qn_ref[...]).astype(BF16)
    ckv_ref[...] = (ckv * lax.rsqrt(jnp.mean(ckv * ckv, -1, keepdims=True) + RMS_EPS) * kvn_ref[...]).astype(BF16)
    rot = pltpu.roll(kp, 96, 1) + pltpu.roll(kp, 32, 1)
    kpe_ref[...] = (kp * cos_ref[...] + rot * sin_ref[...]).astype(BF16)


def mla_norm(proj, q_norm, kv_norm, cos, sin):
    m = proj.shape[0]
    tm = min(512, m)
    return pl.pallas_call(
        _mla_norm_kernel,
        out_shape=(jax.ShapeDtypeStruct((m, MLA_Q_RANK), BF16), jax.ShapeDtypeStruct((m, MLA_KV_RANK), BF16),
                   jax.ShapeDtypeStruct((m, LANES), BF16)),
        grid=(m // tm,),
        in_specs=[pl.BlockSpec((tm, proj.shape[1]), lambda i: (i, 0)),
                  pl.BlockSpec((1, MLA_Q_RANK), lambda i: (0, 0)), pl.BlockSpec((1, MLA_KV_RANK), lambda i: (0, 0)),
                  pl.BlockSpec((tm, LANES), lambda i: (i, 0)), pl.BlockSpec((tm, LANES), lambda i: (i, 0))],
        out_specs=(pl.BlockSpec((tm, MLA_Q_RANK), lambda i: (i, 0)), pl.BlockSpec((tm, MLA_KV_RANK), lambda i: (i, 0)),
                   pl.BlockSpec((tm, LANES), lambda i: (i, 0))),
        compiler_params=_cp("parallel"),
        name="mla_norm",
    )(proj, q_norm, kv_norm, cos, sin)


def _mla_attn_kernel(q_ref, cos_ref, sin_ref, kn_ref, kpe_ref, v_ref, o_ref, q_scr, m_scr, l_scr, acc_scr):
    ki = pl.program_id(2)

    @pl.when(ki == 0)
    def _():
        qp = q_ref[:, MLA_NOPE:]
        rot = pltpu.roll(qp, 96, 1) + pltpu.roll(qp, 32, 1)
        qp = qp * cos_ref[...] + rot * sin_ref[...]
        scale = (MLA_NOPE + MLA_ROPE) ** -0.5
        q_scr[...] = (jnp.concatenate([q_ref[:, :MLA_NOPE], qp], 1) * scale).astype(BF16)
        m_scr[...] = jnp.full_like(m_scr, NEG)
        l_scr[...] = jnp.zeros_like(l_scr)
        acc_scr[...] = jnp.zeros_like(acc_scr)

    k = jnp.concatenate([kn_ref[...], kpe_ref[...]], 1)
    s = _dot_nt(q_scr[...], k)
    m_old = m_scr[...]
    m_new = jnp.maximum(m_old, jnp.max(s, -1, keepdims=True))
    alpha = jnp.exp(m_old - m_new)
    p = jnp.exp(s - m_new)
    l_scr[...] = alpha * l_scr[...] + jnp.sum(p, -1, keepdims=True)
    acc_scr[...] = alpha * acc_scr[...] + _dot(p.astype(BF16), v_ref[...])
    m_scr[...] = m_new

    @pl.when(ki == pl.num_programs(2) - 1)
    def _():
        o_ref[...] = (acc_scr[...] / l_scr[...]).astype(o_ref.dtype)


def mla_attention(q, cos, sin, kv, kpe, tq, tk):
    nq, nk = q.shape[0], kv.shape[0]
    hq = MLA_NOPE + LANES
    return pl.pallas_call(
        _mla_attn_kernel,
        out_shape=jax.ShapeDtypeStruct((nq, MLA_HEADS * MLA_V), BF16),
        grid=(MLA_HEADS, nq // tq, nk // tk),
        in_specs=[pl.BlockSpec((tq, hq), lambda h, i, j: (i, h)),
                  pl.BlockSpec((tq, LANES), lambda h, i, j: (i, 0)),
                  pl.BlockSpec((tq, LANES), lambda h, i, j: (i, 0)),
                  pl.BlockSpec((tk, MLA_NOPE), lambda h, i, j: (j, h)),
                  pl.BlockSpec((tk, LANES), lambda h, i, j: (j, 0)),
                  pl.BlockSpec((tk, MLA_V), lambda h, i, j: (j, MLA_HEADS + h))],
        out_specs=pl.BlockSpec((tq, MLA_V), lambda h, i, j: (i, h)),
        scratch_shapes=[pltpu.VMEM((tq, hq), BF16), pltpu.VMEM((tq, 1), F32), pltpu.VMEM((tq, 1), F32),
                        pltpu.VMEM((tq, MLA_V), F32)],
        compiler_params=_cp("parallel", "parallel", "arbitrary"),
        name="mla_attention",
    )(q, cos, sin, kv, kpe, kv)


def _dwconv_silu_kernel(x_ref, xp_ref, xn_ref, w_ref, b_ref, o_ref):
    i = pl.program_id(0)
    nrow = xp_ref.shape[0]
    x = x_ref[...]
    prev_row = jnp.where(i > 0, xp_ref[nrow - 1:nrow, :], 0.0)
    next_row = jnp.where(i < pl.num_programs(0) - 1, xn_ref[0:1, :], 0.0)
    dn, up = _shift_rows(x, prev_row, next_row)
    w = w_ref[...]
    y = w[0:1, :] * dn + w[1:2, :] * x + w[2:3, :] * up + b_ref[...]
    o_ref[...] = _silu(y)


def dwconv_silu(zx, conv_w, conv_b, col0, ncol):
    n = zx.shape[0]
    tm, tc, halo = min(256, n), 2048, 8
    cb0 = col0 // tc
    nh = n // halo
    return pl.pallas_call(
        _dwconv_silu_kernel,
        out_shape=jax.ShapeDtypeStruct((n, ncol), F32),
        grid=(n // tm, ncol // tc),
        in_specs=[pl.BlockSpec((tm, tc), lambda i, j: (i, cb0 + j)),
                  pl.BlockSpec((halo, tc), lambda i, j: (jnp.maximum(i * (tm // halo) - 1, 0), cb0 + j)),
                  pl.BlockSpec((halo, tc), lambda i, j: (jnp.minimum((i + 1) * (tm // halo), nh - 1), cb0 + j)),
                  pl.BlockSpec((3, tc), lambda i, j: (0, j)),
                  pl.BlockSpec((1, tc), lambda i, j: (0, j))],
        out_specs=pl.BlockSpec((tm, tc), lambda i, j: (i, j)),
        compiler_params=_cp("parallel", "parallel"),
        name="dwconv_silu",
    )(zx, zx, zx, conv_w, conv_b.reshape(1, ncol))


def _ssd_kernel(*refs, reverse, final):
    if final:
        (x_ref, b_ref, c_ref, dt_ref, dtb_ref, alog_ref, s0_ref, z_ref, yfw_ref, dsk_ref, nw_ref,
         y_ref, sfin_ref, st_ref) = refs
    else:
        x_ref, b_ref, c_ref, dt_ref, dtb_ref, alog_ref, s0_ref, y_ref, sfin_ref, st_ref = refs
    ci = pl.program_id(0)

    @pl.when(ci == 0)
    def _():
        st_ref[...] = s0_ref[...]

    n = x_ref.shape[0]
    p, ns = SSM_HEAD_DIM, SSM_STATE
    heads = x_ref.shape[1] // p
    hpg = heads // SSM_GROUPS
    gw = hpg * p
    dt = _softplus(dt_ref[...] + dtb_ref[...])
    a = _cumsum_rows(dt * -jnp.exp(alog_ref[...]), reverse)
    a_t, dt_t = a.T, dt.T
    last = 0 if reverse else n - 1
    a_last = a[last:last + 1, :]
    ti = lax.broadcasted_iota(jnp.int32, (n, n), 0)
    si = lax.broadcasted_iota(jnp.int32, (n, n), 1)
    mask = (si >= ti) if reverse else (si <= ti)
    lane = lax.broadcasted_iota(jnp.int32, (n, LANES), 1)
    lo = lane < p
    lo1 = lax.broadcasted_iota(jnp.int32, (1, LANES), 1) < p
    hoff = heads if reverse else 0
    for g in range(SSM_GROUPS):
        bg = b_ref[:, g * ns:(g + 1) * ns].astype(BF16)
        cg = c_ref[:, g * ns:(g + 1) * ns].astype(BF16)
        cb = _dot_nt(cg, bg)
        cs = _dot(cg, st_ref[:, g * gw:(g + 1) * gw].astype(BF16))
        xw, dec = [], []
        for j in range(hpg // 2):
            h0 = hoff + g * hpg + 2 * j
            sl = slice(g * gw + j * LANES, g * gw + (j + 1) * LANES)
            x2 = x_ref[:, sl]
            x2b = x2.astype(BF16)
            ys = []
            for h in (h0, h0 + 1):
                seg = jnp.where(mask, a[:, h:h + 1] - a_t[h:h + 1, :], NEG)
                w = cb * jnp.exp(seg) * dt_t[h:h + 1, :]
                ys.append(_dot(w.astype(BF16), x2b))
            a_rep = jnp.where(lo, a[:, h0:h0 + 1], a[:, h0 + 1:h0 + 2])
            dt_rep = jnp.where(lo, dt[:, h0:h0 + 1], dt[:, h0 + 1:h0 + 2])
            al_rep = jnp.where(lo1, a_last[:, h0:h0 + 1], a_last[:, h0 + 1:h0 + 2])
            y2 = jnp.where(lo, ys[0], ys[1]) + cs[:, j * LANES:(j + 1) * LANES] * jnp.exp(a_rep)
            xw.append((x2 * (jnp.exp(al_rep - a_rep) * dt_rep)).astype(BF16))
            dec.append(jnp.exp(al_rep))
            if final:
                y2 = y2 + yfw_ref[:, sl] + dsk_ref[:, sl] * x2
                y_ref[:, sl] = y2 * _silu(z_ref[:, sl])
            else:
                y_ref[:, sl] = y2
        gs = slice(g * gw, (g + 1) * gw)
        st_ref[:, gs] = st_ref[:, gs] * jnp.concatenate(dec, 1) + _dot_tn(bg, jnp.concatenate(xw, 1))
        if final:
            yg = y_ref[:, gs]
            y_ref[:, gs] = yg * lax.rsqrt(jnp.mean(yg * yg, -1, keepdims=True) + RMS_EPS) * nw_ref[:, gs]

    @pl.when(ci == pl.num_programs(0) - 1)
    def _():
        sfin_ref[...] = st_ref[...]


def ssd_scan(xbc, zx, dt_col, dt_bias, a_log, s0, reverse, y_fw=None, d_skip=None, norm_w=None):
    n = xbc.shape[0]
    gn = SSM_GROUPS * SSM_STATE
    di = xbc.shape[1] - 2 * gn
    lc = SSM_CHUNK
    nc = n // lc
    final = y_fw is not None
    cidx = (lambda c: nc - 1 - c) if reverse else (lambda c: c)
    wide = lambda col: pl.BlockSpec((lc, di), lambda c: (cidx(c), col))
    vec = lambda w: pl.BlockSpec((1, w), lambda c: (0, 0))
    st = pl.BlockSpec((SSM_STATE, di), lambda c: (0, 0))
    in_specs = [wide(0),
                pl.BlockSpec((lc, gn), lambda c: (cidx(c), di // gn)),
                pl.BlockSpec((lc, gn), lambda c: (cidx(c), di // gn + 1)),
                pl.BlockSpec((lc, LANES), lambda c: (cidx(c), dt_col)),
                vec(LANES), vec(LANES), st]
    args = [xbc, xbc, xbc, zx, dt_bias, a_log, s0]
    if final:
        in_specs += [wide(0), wide(0), vec(di), vec(di)]
        args += [zx, y_fw, d_skip, norm_w]
    return pl.pallas_call(
        functools.partial(_ssd_kernel, reverse=reverse, final=final),
        out_shape=(jax.ShapeDtypeStruct((n, di), F32), jax.ShapeDtypeStruct((SSM_STATE, di), F32)),
        grid=(nc,),
        in_specs=in_specs,
        out_specs=(wide(0), st),
        scratch_shapes=[pltpu.VMEM((SSM_STATE, di), F32)],
        compiler_params=_cp("arbitrary"),
        name="ssd_bw" if reverse else "ssd_fw",
    )(*args)


def _cast_kernel(x_ref, o_ref):
    o_ref[...] = x_ref[...].astype(o_ref.dtype)


def to_bf16(x):
    n, d = x.shape
    tm = min(256, n)
    return pl.pallas_call(
        _cast_kernel,
        out_shape=jax.ShapeDtypeStruct((n, d), BF16),
        grid=(n // tm,),
        in_specs=[pl.BlockSpec((tm, d), lambda i: (i, 0))],
        out_specs=pl.BlockSpec((tm, d), lambda i: (i, 0)),
        compiler_params=_cp("parallel"),
        name="to_bf16",
    )(x)


def _pad_cols(w, n):
    return jnp.pad(w, ((0, 0), (0, n - w.shape[1])))


def swa_mixer(h_ctx, h_lat, w_in, sink, cos2, sin2, with_ctx):
    w = w_in.astype(BF16)
    qkv_l = matmul(h_lat, w, F32, tn=512)
    qkv_c = matmul(h_ctx, w, F32, tn=512)
    o_lat, o_ctx = swa_attention(qkv_l, qkv_c, sink.astype(F32), cos2, sin2, with_ctx)
    return o_ctx, o_lat


def hgrn2_mixer(h_ctx, h_lat, w_in, lower_bound, norm_w, with_ctx):
    w = w_in.astype(BF16)
    hk = HG_HEADS * HG_KDIM
    lb = lower_bound.reshape(1, hk)
    nw = norm_w.reshape(1, hk)
    proj_c = matmul(h_ctx, w, F32, tn=512)
    proj_l = matmul(h_lat, w, F32, tn=512)
    zeros = jnp.zeros((HG_HEADS, HG_KDIM, HG_KDIM), F32)
    ofw_c, s_fw = hgrn2_scan(proj_c, lb, zeros, False)
    y_c, s_bw = hgrn2_scan(proj_c, lb, zeros, True, ofw_c, nw)
    ofw_l, _ = hgrn2_scan(proj_l, lb, s_fw, False)
    y_l, _ = hgrn2_scan(proj_l, lb, s_bw, True, ofw_l, nw)
    return (y_c if with_ctx else None), y_l


def mla_mixer(h_ctx, h_lat, w_in, q_norm, kv_norm, w_uq, w_ukv, cos2, sin2, with_ctx):
    t, nctx = h_lat.shape[0], h_ctx.shape[0]
    hh = MLA_HEADS
    ncol = MLA_Q_RANK + MLA_KV_RANK + LANES
    w1 = _pad_cols(w_in, ncol).astype(BF16)
    wq = w_uq.reshape(MLA_Q_RANK, hh, MLA_NOPE + MLA_ROPE)
    wq = jnp.pad(wq, ((0, 0), (0, 0), (0, LANES - MLA_ROPE))).reshape(MLA_Q_RANK, hh * (MLA_NOPE + LANES)).astype(BF16)
    wkv = w_ukv.reshape(MLA_KV_RANK, hh, MLA_NOPE + MLA_V)
    wkv = jnp.concatenate([wkv[:, :, :MLA_NOPE].reshape(MLA_KV_RANK, hh * MLA_NOPE),
                           wkv[:, :, MLA_NOPE:].reshape(MLA_KV_RANK, hh * MLA_V)], 1).astype(BF16)
    zpad = jnp.zeros((t, LANES - 2 * 32), F32)
    cos_l = jnp.concatenate([cos2, zpad], 1)
    sin_l = jnp.concatenate([sin2, zpad], 1)
    cos_c = jnp.ones((nctx, LANES), F32)
    sin_c = jnp.zeros((nctx, LANES), F32)
    qn, kvn = q_norm.reshape(1, -1), kv_norm.reshape(1, -1)

    def proj(h, cos, sin):
        cq, ckv, kpe = mla_norm(matmul(h, w1, F32, tn=ncol), qn, kvn, cos, sin)
        return matmul(cq, wq, F32, tn=1024), matmul(ckv, wkv, BF16, tn=1024), kpe

    q_c, kv_c, kpe_c = proj(h_ctx, cos_c, sin_c)
    q_l, kv_l, kpe_l = proj(h_lat, cos_l, sin_l)
    o_ctx = mla_attention(q_c, cos_c, sin_c, kv_c, kpe_c, tq=nctx, tk=nctx) if with_ctx else None
    kv_all = jnp.concatenate([kv_l, kv_c], 0)
    kpe_all = jnp.concatenate([kpe_l, kpe_c], 0)
    o_lat = mla_attention(q_l, cos_l, sin_l, kv_all, kpe_all, tq=512, tk=1280)
    return o_ctx, o_lat


def mamba2_mixer(h_ctx, h_lat, w_in, conv_w, conv_b, dt_bias, a_log, d_skip, norm_w, with_ctx):
    di = norm_w.shape[0]
    gn = SSM_GROUPS * SSM_STATE
    heads = di // SSM_HEAD_DIM
    ncv = di + 2 * gn
    w = w_in.astype(BF16)
    dtb = dt_bias.reshape(1, 2 * heads).astype(F32)
    alog = a_log.reshape(1, 2 * heads).astype(F32)
    dsk = jnp.repeat(d_skip.astype(F32), SSM_HEAD_DIM).reshape(1, di)
    nw = norm_w.reshape(1, di)
    dt_col = (di + ncv) // LANES

    def prep(h):
        zx = matmul(h, w, F32, tn=1152)
        return zx, dwconv_silu(zx, conv_w, conv_b, di, ncv)

    zx_c, xbc_c = prep(h_ctx)
    zx_l, xbc_l = prep(h_lat)
    zeros = jnp.zeros((SSM_STATE, di), F32)
    yfw_c, s_fw = ssd_scan(xbc_c, zx_c, dt_col, dtb, alog, zeros, False)
    y_c, s_bw = ssd_scan(xbc_c, zx_c, dt_col, dtb, alog, zeros, True, yfw_c, dsk, nw)
    yfw_l, _ = ssd_scan(xbc_l, zx_l, dt_col, dtb, alog, s_fw, False)
    y_l, _ = ssd_scan(xbc_l, zx_l, dt_col, dtb, alog, s_bw, True, yfw_l, dsk, nw)
    return (to_bf16(y_c) if with_ctx else None), to_bf16(y_l)


def kernel(x, c, ctx, c_ctx, ada_w, ada_b, ln_g, ln_b, ffn_up, ffn_conv_w, ffn_conv_b, ffn_down, swa_in, swa_sink, swa_out, hg_in, hg_lb_logits, hg_norm_w, hg_out, mla_in, mla_q_norm, mla_kv_norm, mla_uq, mla_ukv, mla_out, ssm_in, ssm_conv_w, ssm_conv_b, ssm_dt_bias, ssm_A_log, ssm_D, ssm_norm_w, ssm_out):
    batch, t, d = x.shape
    assert batch == 1
    depth = ada_w.shape[0]
    x_lat, x_ctx = x[0], ctx[0]

    probs = jax.nn.softmax(hg_lb_logits.astype(F32), axis=0)
    lower_bounds = jnp.cumsum(probs, axis=0) - probs[0]

    cvec = jnp.zeros((8, d), F32).at[0].set(c[0]).at[1].set(c_ctx)
    mods = ada_modulation(cvec, ada_w, ada_b)

    def mod(i, stream, j):
        return mods[i, stream:stream + 1, j * d:(j + 1) * d]

    cos2, sin2 = rope_tables(t)
    zero = jnp.zeros((1, d), F32)
    h_lat = modulate(x_lat, mod(0, 0, 0), mod(0, 0, 1))
    h_ctx = modulate(x_ctx, mod(0, 1, 0), mod(0, 1, 1))
    for i in range(depth):
        kind, j = i % N_MIXERS, i // N_MIXERS
        with_ctx = i < depth - 1
        if kind == 0:
            y_ctx, y_lat = swa_mixer(h_ctx, h_lat, swa_in[j], swa_sink[j], cos2, sin2, with_ctx)
            w_out = swa_out[j]
        elif kind == 1:
            y_ctx, y_lat = hgrn2_mixer(h_ctx, h_lat, hg_in[j], lower_bounds[i], hg_norm_w[j], with_ctx)
            w_out = hg_out[j]
        elif kind == 2:
            y_ctx, y_lat = mla_mixer(h_ctx, h_lat, mla_in[j], mla_q_norm[j], mla_kv_norm[j], mla_uq[j], mla_ukv[j],
                                     cos2, sin2, with_ctx)
            w_out = mla_out[j]
        else:
            y_ctx, y_lat = mamba2_mixer(h_ctx, h_lat, ssm_in[j], ssm_conv_w[j], ssm_conv_b[j], ssm_dt_bias[j],
                                        ssm_A_log[j], ssm_D[j], ssm_norm_w[j], with_ctx)
            w_out = ssm_out[j]
        w_out = w_out.astype(BF16)
        w_up, w_down = ffn_up[i].astype(BF16), ffn_down[i].astype(BF16)
        g0, b0, g1, b1 = ln_g[i, 0:1], ln_b[i, 0:1], ln_g[i, 1:2], ln_b[i, 1:2]
        streams = [(0, x_lat, y_lat)] + ([(1, x_ctx, y_ctx)] if with_ctx else [])
        outs = []
        for s, xs, ys in streams:
            xs, hf = matmul_ln(ys, w_out, xs, mod(i, s, 2), g0, b0, mod(i, s, 3), mod(i, s, 4))
            nsh, nsc = (mod(i + 1, s, 0), mod(i + 1, s, 1)) if i + 1 < depth else (zero, zero)
            outs.append(conv_ffn_ln(hf, w_up, ffn_conv_w[i], ffn_conv_b[i], w_down, xs, mod(i, s, 5), g1, b1, nsh, nsc))
        x_lat, h_lat = outs[0]
        if with_ctx:
            x_ctx, h_ctx = outs[1]
    return x_lat[None]
```

```python
import functools

import jax
import jax.numpy as jnp
from jax import lax
from jax.experimental import pallas as pl
from jax.experimental.pallas import tpu as pltpu

F32, BF16 = jnp.float32, jnp.bfloat16

DEPTH, N_MIXERS = 4, 4
GRID_W = 64
ALPHA = (2.0 * DEPTH) ** 0.25
LN_EPS, RMS_EPS, ROPE_BASE = 1e-5, 1e-6, 10000.0
SWA_HEADS, SWA_KV_HEADS, SWA_HEAD_DIM, SWA_BLOCK = 32, 4, 64, 128
HG_HEADS, HG_KDIM = 16, 128
MLA_HEADS, MLA_Q_RANK, MLA_KV_RANK, MLA_NOPE, MLA_ROPE, MLA_V = 16, 512, 512, 128, 64, 128
SSM_HEAD_DIM, SSM_STATE, SSM_GROUPS, SSM_CHUNK = 64, 128, 8, 128
HG_CHUNK = 64
LANES = 128
NEG = -1e30
LOG2E = 1.4426950408889634
VMEM_LIMIT = 56 * 1024 * 1024


def _cp(*sem):
    return pltpu.CompilerParams(dimension_semantics=sem, vmem_limit_bytes=VMEM_LIMIT)


def _dot(a, b):
    return jnp.dot(a, b, preferred_element_type=F32)


def _dot_nt(a, b):
    return lax.dot_general(a, b, (((1,), (1,)), ((), ())), preferred_element_type=F32)


def _dot_tn(a, b):
    return lax.dot_general(a, b, (((0,), (0,)), ((), ())), preferred_element_type=F32)


def _sigmoid(x):
    return 1.0 / (1.0 + jnp.exp(-x))


def _silu(x):
    return x * _sigmoid(x)


def _softplus(x):
    return jnp.maximum(x, 0.0) + jnp.log(1.0 + jnp.exp(-jnp.abs(x)))


def _ln_mod(z, g, b):
    mu = jnp.mean(z, -1, keepdims=True)
    d = z - mu
    var = jnp.mean(d * d, -1, keepdims=True)
    return d * lax.rsqrt(var + LN_EPS) * g + b


def _shift_rows(x, prev_row, next_row):
    n = x.shape[0]
    row = lax.broadcasted_iota(jnp.int32, (n, 1), 0)
    dn = jnp.where(row == 0, prev_row, pltpu.roll(x, 1, 0))
    up = jnp.where(row == n - 1, next_row, pltpu.roll(x, n - 1, 0))
    return dn, up


def _cumsum_rows(x, reverse):
    n = x.shape[0]
    row = lax.broadcasted_iota(jnp.int32, (n, 1), 0)
    s = 1
    while s < n:
        if reverse:
            x = x + jnp.where(row < n - s, pltpu.roll(x, n - s, 0), 0.0)
        else:
            x = x + jnp.where(row >= s, pltpu.roll(x, s, 0), 0.0)
        s *= 2
    return x


def _ada_kernel(c_ref, w_ref, b_ref, o_ref):
    a = _silu(c_ref[...]).astype(BF16)
    o_ref[0] = _dot(a, w_ref[0].astype(BF16)) + b_ref[0]


def ada_modulation(cvec, ada_w, ada_b):
    depth, d, n = ada_w.shape
    tn = 1024
    return pl.pallas_call(
        _ada_kernel,
        out_shape=jax.ShapeDtypeStruct((depth, 8, n), F32),
        grid=(depth, n // tn),
        in_specs=[pl.BlockSpec((8, d), lambda i, j: (0, 0)),
                  pl.BlockSpec((1, d, tn), lambda i, j: (i, 0, j)),
                  pl.BlockSpec((1, 1, tn), lambda i, j: (i, 0, j))],
        out_specs=pl.BlockSpec((1, 8, tn), lambda i, j: (i, 0, j)),
        compiler_params=_cp("parallel", "parallel"),
        name="ada_modulation",
    )(cvec, ada_w, ada_b.reshape(depth, 1, n))


def _modulate_kernel(x_ref, sh_ref, sc_ref, o_ref):
    o_ref[...] = (x_ref[...] * (1.0 + sc_ref[...]) + sh_ref[...]).astype(o_ref.dtype)


def modulate(x, shift, scale):
    m, d = x.shape
    tm = min(512, m)
    vec = pl.BlockSpec((1, d), lambda i: (0, 0))
    return pl.pallas_call(
        _modulate_kernel,
        out_shape=jax.ShapeDtypeStruct((m, d), BF16),
        grid=(m // tm,),
        in_specs=[pl.BlockSpec((tm, d), lambda i: (i, 0)), vec, vec],
        out_specs=pl.BlockSpec((tm, d), lambda i: (i, 0)),
        compiler_params=_cp("parallel"),
        name="modulate",
    )(x, shift, scale)


def _mm_kernel(a_ref, w_ref, o_ref):
    o_ref[...] = _dot(a_ref[...], w_ref[...]).astype(o_ref.dtype)


def matmul(a, w, out_dtype, tn, tm=1024):
    m, k = a.shape
    n = w.shape[1]
    tm = min(tm, m)
    assert m % tm == 0 and n % tn == 0
    return pl.pallas_call(
        _mm_kernel,
        out_shape=jax.ShapeDtypeStruct((m, n), out_dtype),
        grid=(m // tm, n // tn),
        in_specs=[pl.BlockSpec((tm, k), lambda i, j: (i, 0)),
                  pl.BlockSpec((k, tn), lambda i, j: (0, j))],
        out_specs=pl.BlockSpec((tm, tn), lambda i, j: (i, j)),
        compiler_params=_cp("parallel", "parallel"),
        name="matmul",
    )(a, w)


def _mm_ln_kernel(a_ref, w_ref, x_ref, gate_ref, g_ref, b_ref, sh_ref, sc_ref, xo_ref, ho_ref, acc_ref):
    k = pl.program_id(1)

    @pl.when(k == 0)
    def _():
        acc_ref[...] = jnp.zeros_like(acc_ref)

    acc_ref[...] += _dot(a_ref[...], w_ref[...])

    @pl.when(k == pl.num_programs(1) - 1)
    def _():
        xn = _ln_mod(ALPHA * x_ref[...] + gate_ref[...] * acc_ref[...], g_ref[...], b_ref[...])
        xo_ref[...] = xn
        ho_ref[...] = (xn * (1.0 + sc_ref[...]) + sh_ref[...]).astype(ho_ref.dtype)


def matmul_ln(a, w, x, gate, g, b, shift, scale):
    m, k = a.shape
    d = w.shape[1]
    tm, tk = min(256, m), 2048
    vec = pl.BlockSpec((1, d), lambda i, kk: (0, 0))
    row = pl.BlockSpec((tm, d), lambda i, kk: (i, 0))
    return pl.pallas_call(
        _mm_ln_kernel,
        out_shape=(jax.ShapeDtypeStruct((m, d), F32), jax.ShapeDtypeStruct((m, d), BF16)),
        grid=(m // tm, k // tk),
        in_specs=[pl.BlockSpec((tm, tk), lambda i, kk: (i, kk)),
                  pl.BlockSpec((tk, d), lambda i, kk: (kk, 0)),
                  row, vec, vec, vec, vec, vec],
        out_specs=(row, row),
        scratch_shapes=[pltpu.VMEM((tm, d), F32)],
        compiler_params=_cp("parallel", "arbitrary"),
        name="matmul_ln",
    )(a, w, x, gate, g, b, shift, scale)


def _ffn_kernel(h_ref, hp_ref, hn_ref, wa_ref, wv_ref, cw_ref, cb_ref, wd_ref, x_ref, gate_ref, g_ref, b_ref,
                sh_ref, sc_ref, xo_ref, ho_ref, acc_ref):
    i, f = pl.program_id(0), pl.program_id(1)

    @pl.when(f == 0)
    def _():
        acc_ref[...] = jnp.zeros_like(acc_ref)

    wa = wa_ref[...]
    nrow = hp_ref.shape[0]
    a_prev = _dot(hp_ref[...], wa)[nrow - 1:nrow, :]
    a_next = _dot(hn_ref[...], wa)[0:1, :]
    a_prev = jnp.where(i > 0, a_prev, 0.0)
    a_next = jnp.where(i < pl.num_programs(0) - 1, a_next, 0.0)
    cw = cw_ref[...]
    a = _dot(h_ref[...], wa)
    v = _dot(h_ref[...], wv_ref[...])
    dn, up = _shift_rows(a, a_prev, a_next)
    cv = cw[0:1, :] * dn + cw[1:2, :] * a + cw[2:3, :] * up + cb_ref[...]
    u = (_silu(cv) * v).astype(BF16)
    acc_ref[...] += _dot(u, wd_ref[...])

    @pl.when(f == pl.num_programs(1) - 1)
    def _():
        xn = _ln_mod(ALPHA * x_ref[...] + gate_ref[...] * acc_ref[...], g_ref[...], b_ref[...])
        xo_ref[...] = xn
        ho_ref[...] = (xn * (1.0 + sc_ref[...]) + sh_ref[...]).astype(ho_ref.dtype)


def conv_ffn_ln(h, w_up, conv_w, conv_b, w_down, x, gate, g, b, shift, scale):
    m, d = h.shape
    f = w_down.shape[0]
    tm, tf, halo = min(512, m), 512, 16
    nf = f // tf
    nh = m // halo
    vec = pl.BlockSpec((1, d), lambda i, j: (0, 0))
    row = pl.BlockSpec((tm, d), lambda i, j: (i, 0))
    return pl.pallas_call(
        _ffn_kernel,
        out_shape=(jax.ShapeDtypeStruct((m, d), F32), jax.ShapeDtypeStruct((m, d), BF16)),
        grid=(m // tm, nf),
        in_specs=[row,
                  pl.BlockSpec((halo, d), lambda i, j: (jnp.maximum(i * (tm // halo) - 1, 0), 0)),
                  pl.BlockSpec((halo, d), lambda i, j: (jnp.minimum((i + 1) * (tm // halo), nh - 1), 0)),
                  pl.BlockSpec((d, tf), lambda i, j: (0, j)),
                  pl.BlockSpec((d, tf), lambda i, j: (0, nf + j)),
                  pl.BlockSpec((3, tf), lambda i, j: (0, j)),
                  pl.BlockSpec((1, tf), lambda i, j: (0, j)),
                  pl.BlockSpec((tf, d), lambda i, j: (j, 0)),
                  row, vec, vec, vec, vec, vec],
        out_specs=(row, row),
        scratch_shapes=[pltpu.VMEM((tm, d), F32)],
        compiler_params=_cp("parallel", "arbitrary"),
        name="conv_ffn_ln",
    )(h, h, h, w_up, w_up, conv_w, conv_b.reshape(1, f), w_down, x, gate, g, b, shift, scale)


def rope_tables(n_tokens):
    rows = n_tokens // GRID_W
    row = jnp.repeat(jnp.arange(rows, dtype=F32), GRID_W)
    col = jnp.tile(jnp.arange(GRID_W, dtype=F32), rows)
    n_axis = 16
    inv_freq = ROPE_BASE ** (-jnp.arange(n_axis, dtype=F32) / n_axis)
    ang = jnp.concatenate([row[:, None] * inv_freq, col[:, None] * inv_freq], -1)
    cos, sin = jnp.cos(ang), jnp.sin(ang)
    return jnp.concatenate([cos, cos], -1), jnp.concatenate([-sin, sin], -1)


def _rope2(x, cos, sin):
    lane = lax.broadcasted_iota(jnp.int32, x.shape, 1)
    rot = jnp.where((lane & 63) < 32, pltpu.roll(x, 96, 1), pltpu.roll(x, 32, 1))
    return x * cos + rot * sin


def _dup_half(x, upper):
    lane = lax.broadcasted_iota(jnp.int32, x.shape, 1)
    keep = (lane >= 64) if upper else (lane < 64)
    return jnp.where(keep, x, pltpu.roll(x, 64, 1))


def _swa_pairs(sink_ref, q_ref, keys, vals, bias, o_ref, rope):
    blk = q_ref.shape[0]
    lane = lax.broadcasted_iota(jnp.int32, (blk, LANES), 1)
    row2 = lax.broadcasted_iota(jnp.int32, (2 * blk, 1), 0)
    pairs_per_group = SWA_HEADS // SWA_KV_HEADS // 2
    for hp in range(SWA_HEADS // 2):
        g = hp // pairs_per_group
        q2 = q_ref[:, hp * LANES:(hp + 1) * LANES]
        if rope is not None:
            q2 = _rope2(q2, *rope)
        q2 = q2 * SWA_HEAD_DIM ** -0.5
        qq = jnp.concatenate([jnp.where(lane < 64, q2, 0.0), jnp.where(lane >= 64, q2, 0.0)], 0).astype(BF16)
        s = _dot_nt(qq, keys[g])
        if bias is not None:
            s = s + bias
        sink = jnp.where(row2 < blk, sink_ref[2 * hp], sink_ref[2 * hp + 1])
        m = jnp.maximum(jnp.max(s, -1, keepdims=True), sink)
        p = jnp.exp(s - m)
        l = jnp.sum(p, -1, keepdims=True) + jnp.exp(sink - m)
        o2 = _dot(p.astype(BF16), vals[g]) / l
        o_ref[:, hp * LANES:(hp + 1) * LANES] = jnp.where(lane < 64, o2[:blk], o2[blk:]).astype(o_ref.dtype)


def _swa_lat_kernel(sink_ref, q_ref, kp_ref, kc_ref, kn_ref, vp_ref, vc_ref, vn_ref, kx_ref, vx_ref,
                    cq_ref, sq_ref, cp_ref, sp_ref, cn_ref, sn_ref, o_ref):
    i, nb = pl.program_id(0), pl.num_programs(0)
    blk = q_ref.shape[0]
    nctx = kx_ref.shape[0]
    keys, vals = [], []
    for g in range(SWA_KV_HEADS):
        sl = slice((g // 2) * LANES, (g // 2 + 1) * LANES)
        up = g % 2 == 1
        kcat = jnp.concatenate([
            _dup_half(_rope2(kp_ref[:, sl], cp_ref[...], sp_ref[...]), up),
            _dup_half(_rope2(kc_ref[:, sl], cq_ref[...], sq_ref[...]), up),
            _dup_half(_rope2(kn_ref[:, sl], cn_ref[...], sn_ref[...]), up),
            _dup_half(kx_ref[:, sl], up)], 0)
        vcat = jnp.concatenate([_dup_half(vp_ref[:, sl], up), _dup_half(vc_ref[:, sl], up),
                                _dup_half(vn_ref[:, sl], up), _dup_half(vx_ref[:, sl], up)], 0)
        keys.append(kcat.astype(BF16))
        vals.append(vcat.astype(BF16))
    nk = 3 * blk + nctx
    r = lax.broadcasted_iota(jnp.int32, (2 * blk, nk), 0) & (blk - 1)
    c = lax.broadcasted_iota(jnp.int32, (2 * blk, nk), 1)
    has_prev = jnp.where(i > 0, 0.0, NEG)
    has_next = jnp.where(i < nb - 1, 0.0, NEG)
    bias = jnp.where(c < blk, jnp.where(c >= r, has_prev, NEG),
                     jnp.where(c < 2 * blk, 0.0,
                               jnp.where(c < 3 * blk, jnp.where(c - 2 * blk <= r, has_next, NEG), 0.0)))
    _swa_pairs(sink_ref, q_ref, keys, vals, bias, o_ref, (cq_ref[...], sq_ref[...]))


def _swa_ctx_kernel(sink_ref, q_ref, kx_ref, vx_ref, o_ref):
    keys, vals = [], []
    for g in range(SWA_KV_HEADS):
        sl = slice((g // 2) * LANES, (g // 2 + 1) * LANES)
        keys.append(_dup_half(kx_ref[:, sl], g % 2 == 1).astype(BF16))
        vals.append(_dup_half(vx_ref[:, sl], g % 2 == 1).astype(BF16))
    _swa_pairs(sink_ref, q_ref, keys, vals, None, o_ref, None)


def swa_attention(qkv_lat, qkv_ctx, sink, cos2, sin2, with_ctx):
    t, nctx = qkv_lat.shape[0], qkv_ctx.shape[0]
    blk, dq = SWA_BLOCK, SWA_HEADS * SWA_HEAD_DIM
    dkv = SWA_KV_HEADS * SWA_HEAD_DIM
    kcol, vcol = dq // dkv, dq // dkv + 1
    nb = t // blk
    cos = jnp.concatenate([cos2, cos2], -1)
    sin = jnp.concatenate([sin2, sin2], -1)
    smem = pl.BlockSpec(memory_space=pltpu.SMEM)
    prev = lambda i: jnp.maximum(i - 1, 0)
    nxt = lambda i: jnp.minimum(i + 1, nb - 1)
    tab = lambda f: pl.BlockSpec((blk, LANES), lambda i: (f(i), 0))
    kv = lambda f, col: pl.BlockSpec((blk, dkv), lambda i: (f(i), col))
    same = lambda i: i
    o_lat = pl.pallas_call(
        _swa_lat_kernel,
        out_shape=jax.ShapeDtypeStruct((t, dq), BF16),
        grid=(nb,),
        in_specs=[smem, pl.BlockSpec((blk, dq), lambda i: (i, 0)),
                  kv(prev, kcol), kv(same, kcol), kv(nxt, kcol), kv(prev, vcol), kv(same, vcol), kv(nxt, vcol),
                  pl.BlockSpec((nctx, dkv), lambda i: (0, kcol)), pl.BlockSpec((nctx, dkv), lambda i: (0, vcol)),
                  tab(same), tab(same), tab(prev), tab(prev), tab(nxt), tab(nxt)],
        out_specs=pl.BlockSpec((blk, dq), lambda i: (i, 0)),
        compiler_params=_cp("parallel"),
        name="swa_lat",
    )(sink, qkv_lat, qkv_lat, qkv_lat, qkv_lat, qkv_lat, qkv_lat, qkv_lat, qkv_ctx, qkv_ctx,
      cos, sin, cos, sin, cos, sin)
    if not with_ctx:
        return o_lat, None
    o_ctx = pl.pallas_call(
        _swa_ctx_kernel,
        out_shape=jax.ShapeDtypeStruct((nctx, dq), BF16),
        grid=(nctx // blk,),
        in_specs=[smem, pl.BlockSpec((blk, dq), lambda i: (i, 0)),
                  pl.BlockSpec((nctx, dkv), lambda i: (0, kcol)), pl.BlockSpec((nctx, dkv), lambda i: (0, vcol))],
        out_specs=pl.BlockSpec((blk, dq), lambda i: (i, 0)),
        compiler_params=_cp("parallel"),
        name="swa_ctx",
    )(sink, qkv_ctx, qkv_ctx, qkv_ctx)
    return o_lat, o_ctx


def _hg_kernel(*refs, reverse, final):
    if final:
        q_ref, f_ref, v_ref, lb_ref, s0_ref, g_ref, ofw_ref, nw_ref, o_ref, sfin_ref, st_ref = refs
    else:
        q_ref, f_ref, v_ref, lb_ref, s0_ref, o_ref, sfin_ref, st_ref = refs
    c = pl.program_id(0)

    @pl.when(c == 0)
    def _():
        st_ref[...] = s0_ref[...]

    n = q_ref.shape[0]
    kd = HG_KDIM
    lb = lb_ref[...]
    q = _silu(q_ref[...]) * kd ** -0.5
    sg = _sigmoid(f_ref[...])
    lf = jnp.log(lb + (1.0 - lb) * sg)
    k = (1.0 - lb) * (1.0 - sg)
    v = v_ref[...].astype(BF16)
    b = _cumsum_rows(lf, reverse)
    mid = n // 2 if reverse else n // 2 - 1
    last = 0 if reverse else n - 1
    r = b[mid:mid + 1, :]
    b_last = b[last:last + 1, :]
    qe = (q * jnp.exp(b - r)).astype(BF16)
    ke = (k * jnp.exp(r - b)).astype(BF16)
    qs = (q * jnp.exp(b)).astype(BF16)
    kl = (k * jnp.exp(b_last - b)).astype(BF16)
    dec = jnp.exp(b_last)
    ti = lax.broadcasted_iota(jnp.int32, (n, n), 0)
    si = lax.broadcasted_iota(jnp.int32, (n, n), 1)
    mask = (si >= ti) if reverse else (si <= ti)
    for h in range(q_ref.shape[1] // kd):
        sl = slice(h * kd, (h + 1) * kd)
        att = jnp.where(mask, _dot_nt(qe[:, sl], ke[:, sl]), 0.0).astype(BF16)
        st = st_ref[h]
        o = _dot(att, v[:, sl]) + _dot_nt(qs[:, sl], st.astype(BF16))
        st_ref[h] = st * dec[:, sl] + _dot_tn(v[:, sl], kl[:, sl])
        if final:
            o = o + ofw_ref[:, sl]
            o = o * lax.rsqrt(jnp.mean(o * o, -1, keepdims=True) + RMS_EPS) * nw_ref[:, sl]
            o_ref[:, sl] = (o * _silu(g_ref[:, sl])).astype(o_ref.dtype)
        else:
            o_ref[:, sl] = o

    @pl.when(c == pl.num_programs(0) - 1)
    def _():
        sfin_ref[...] = st_ref[...]


def hgrn2_scan(proj, lb, s0, reverse, o_fw=None, norm_w=None):
    n = proj.shape[0]
    hk = HG_HEADS * HG_KDIM
    lc = HG_CHUNK
    nc = n // lc
    final = o_fw is not None
    cidx = (lambda c: nc - 1 - c) if reverse else (lambda c: c)
    blk = lambda col: pl.BlockSpec((lc, hk), lambda c: (cidx(c), col))
    vec = pl.BlockSpec((1, hk), lambda c: (0, 0))
    st = pl.BlockSpec((HG_HEADS, HG_KDIM, HG_KDIM), lambda c: (0, 0, 0))
    in_specs = [blk(0), blk(2 if reverse else 1), blk(3), vec, st]
    args = [proj, proj, proj, lb, s0]
    if final:
        in_specs += [blk(4), blk(0), vec]
        args += [proj, o_fw, norm_w]
    return pl.pallas_call(
        functools.partial(_hg_kernel, reverse=reverse, final=final),
        out_shape=(jax.ShapeDtypeStruct((n, hk), BF16 if final else F32),
                   jax.ShapeDtypeStruct((HG_HEADS, HG_KDIM, HG_KDIM), F32)),
        grid=(nc,),
        in_specs=in_specs,
        out_specs=(blk(0), st),
        scratch_shapes=[pltpu.VMEM((HG_HEADS, HG_KDIM, HG_KDIM), F32)],
        compiler_params=_cp("arbitrary"),
        name="hgrn2_bw" if reverse else "hgrn2_fw",
    )(*args)


def _mla_norm_kernel(p_ref, qn_ref, kvn_ref, cos_ref, sin_ref, cq_ref, ckv_ref, kpe_ref):
    cq = p_ref[:, :MLA_Q_RANK]
    ckv = p_ref[:, MLA_Q_RANK:MLA_Q_RANK + MLA_KV_RANK]
    kp = p_ref[:, MLA_Q_RANK + MLA_KV_RANK:]
    cq_ref[...] = (cq * lax.rsqrt(jnp.mean(cq * cq, -1, keepdims=True) + RMS_EPS) * qn_ref[...]).astype(BF16)
    ckv_ref[...] = (ckv * lax.rsqrt(jnp.mean(ckv * ckv, -1, keepdims=True) + RMS_EPS) * kvn_ref[...]).astype(BF16)
    rot = pltpu.roll(kp, 96, 1) + pltpu.roll(kp, 32, 1)
    kpe_ref[...] = (kp * cos_ref[...] + rot * sin_ref[...]).astype(BF16)


def mla_norm(proj, q_norm, kv_norm, cos, sin):
    m = proj.shape[0]
    tm = min(512, m)
    return pl.pallas_call(
        _mla_norm_kernel,
        out_shape=(jax.ShapeDtypeStruct((m, MLA_Q_RANK), BF16), jax.ShapeDtypeStruct((m, MLA_KV_RANK), BF16),
                   jax.ShapeDtypeStruct((m, LANES), BF16)),
        grid=(m // tm,),
        in_specs=[pl.BlockSpec((tm, proj.shape[1]), lambda i: (i, 0)),
                  pl.BlockSpec((1, MLA_Q_RANK), lambda i: (0, 0)), pl.BlockSpec((1, MLA_KV_RANK), lambda i: (0, 0)),
                  pl.BlockSpec((tm, LANES), lambda i: (i, 0)), pl.BlockSpec((tm, LANES), lambda i: (i, 0))],
        out_specs=(pl.BlockSpec((tm, MLA_Q_RANK), lambda i: (i, 0)), pl.BlockSpec((tm, MLA_KV_RANK), lambda i: (i, 0)),
                   pl.BlockSpec((tm, LANES), lambda i: (i, 0))),
        compiler_params=_cp("parallel"),
        name="mla_norm",
    )(proj, q_norm, kv_norm, cos, sin)


def _mla_attn_kernel(q_ref, cos_ref, sin_ref, kn_ref, kpe_ref, v_ref, o_ref,
                     q_scr, s0_scr, s1_scr, p0_scr, p1_scr, a0_scr, a1_scr, m_scr, acc_scr, *, tk):
    n = kn_ref.shape[0] // tk
    s_scr, p_scr, a_scr = (s0_scr, s1_scr), (p0_scr, p1_scr), (a0_scr, a1_scr)

    qp = q_ref[:, MLA_NOPE:]
    rot = pltpu.roll(qp, 96, 1) + pltpu.roll(qp, 32, 1)
    qp = qp * cos_ref[...] + rot * sin_ref[...]
    scale = (MLA_NOPE + MLA_ROPE) ** -0.5 * LOG2E
    q_scr[...] = (jnp.concatenate([q_ref[:, :MLA_NOPE], qp], 1) * scale).astype(BF16)
    m_scr[...] = jnp.full_like(m_scr, NEG)
    acc_scr[...] = jnp.zeros_like(acc_scr)
    ones = jnp.ones((tk, LANES), BF16)

    def rows(j):
        return pl.ds(j * tk, tk) if isinstance(j, int) else pl.ds(pl.multiple_of(j * tk, tk), tk)

    def scores(j, slot):
        k = jnp.concatenate([kn_ref[rows(j), :], kpe_ref[rows(j), :]], 1)
        s_scr[slot][...] = _dot_nt(q_scr[...], k)

    def softmax(slot):
        s = s_scr[slot][...]
        m_old = m_scr[...]
        m_new = jnp.maximum(m_old, jnp.max(s, -1, keepdims=True))
        p_scr[slot][...] = jnp.exp2(s - m_new).astype(BF16)
        a_scr[slot][...] = jnp.exp2(m_old - m_new)
        m_scr[...] = m_new

    def weighted(j, slot):
        v1 = jnp.concatenate([v_ref[rows(j), :], ones], 1)
        acc_scr[...] = a_scr[slot][...] * acc_scr[...] + _dot(p_scr[slot][...], v1)

    def steady(t, a, b):
        scores(t + 1, a)
        softmax(b)
        weighted(t - 1, a)

    scores(0, 0)
    if n > 1:
        scores(1, 1)
        softmax(0)
        pairs = (n - 2) // 2

        def pair(i, carry):
            steady(1 + 2 * i, 0, 1)
            steady(2 + 2 * i, 1, 0)
            return carry

        if pairs > 0:
            lax.fori_loop(0, pairs, pair, 0)
        if (n - 2) % 2:
            steady(n - 2, 0, 1)
        weighted(n - 2, (n - 2) % 2)
    softmax((n - 1) % 2)
    weighted(n - 1, (n - 1) % 2)
    o_ref[...] = (acc_scr[:, :MLA_V] / acc_scr[:, MLA_V:]).astype(o_ref.dtype)


def mla_attention(q, cos, sin, kv, kpe, tq, tk):
    nq, nk = q.shape[0], kv.shape[0]
    hq = MLA_NOPE + LANES
    assert nk % tk == 0 and nq % tq == 0
    return pl.pallas_call(
        functools.partial(_mla_attn_kernel, tk=tk),
        out_shape=jax.ShapeDtypeStruct((nq, MLA_HEADS * MLA_V), BF16),
        grid=(MLA_HEADS, nq // tq),
        in_specs=[pl.BlockSpec((tq, hq), lambda h, i: (i, h)),
                  pl.BlockSpec((tq, LANES), lambda h, i: (i, 0)),
                  pl.BlockSpec((tq, LANES), lambda h, i: (i, 0)),
                  pl.BlockSpec((nk, MLA_NOPE), lambda h, i: (0, h)),
                  pl.BlockSpec((nk, LANES), lambda h, i: (0, 0)),
                  pl.BlockSpec((nk, MLA_V), lambda h, i: (0, MLA_HEADS + h))],
        out_specs=pl.BlockSpec((tq, MLA_V), lambda h, i: (i, h)),
        scratch_shapes=[pltpu.VMEM((tq, hq), BF16),
                        pltpu.VMEM((tq, tk), F32), pltpu.VMEM((tq, tk), F32),
                        pltpu.VMEM((tq, tk), BF16), pltpu.VMEM((tq, tk), BF16),
                        pltpu.VMEM((tq, 1), F32), pltpu.VMEM((tq, 1), F32), pltpu.VMEM((tq, 1), F32),
                        pltpu.VMEM((tq, MLA_V + LANES), F32)],
        compiler_params=_cp("parallel", "parallel"),
        name="mla_attention",
    )(q, cos, sin, kv, kpe, kv)


def _dwconv_silu_kernel(x_ref, xp_ref, xn_ref, w_ref, b_ref, o_ref):
    i = pl.program_id(0)
    nrow = xp_ref.shape[0]
    x = x_ref[...]
    prev_row = jnp.where(i > 0, xp_ref[nrow - 1:nrow, :], 0.0)
    next_row = jnp.where(i < pl.num_programs(0) - 1, xn_ref[0:1, :], 0.0)
    dn, up = _shift_rows(x, prev_row, next_row)
    w = w_ref[...]
    y = w[0:1, :] * dn + w[1:2, :] * x + w[2:3, :] * up + b_ref[...]
    o_ref[...] = _silu(y)


def dwconv_silu(zx, conv_w, conv_b, col0, ncol):
    n = zx.shape[0]
    tm, tc, halo = min(256, n), 2048, 8
    cb0 = col0 // tc
    nh = n // halo
    return pl.pallas_call(
        _dwconv_silu_kernel,
        out_shape=jax.ShapeDtypeStruct((n, ncol), F32),
        grid=(n // tm, ncol // tc),
        in_specs=[pl.BlockSpec((tm, tc), lambda i, j: (i, cb0 + j)),
                  pl.BlockSpec((halo, tc), lambda i, j: (jnp.maximum(i * (tm // halo) - 1, 0), cb0 + j)),
                  pl.BlockSpec((halo, tc), lambda i, j: (jnp.minimum((i + 1) * (tm // halo), nh - 1), cb0 + j)),
                  pl.BlockSpec((3, tc), lambda i, j: (0, j)),
                  pl.BlockSpec((1, tc), lambda i, j: (0, j))],
        out_specs=pl.BlockSpec((tm, tc), lambda i, j: (i, j)),
        compiler_params=_cp("parallel", "parallel"),
        name="dwconv_silu",
    )(zx, zx, zx, conv_w, conv_b.reshape(1, ncol))


def _ssd_kernel(*refs, reverse, final):
    if final:
        (x_ref, b_ref, c_ref, dt_ref, dtb_ref, alog_ref, s0_ref, z_ref, yfw_ref, dsk_ref, nw_ref,
         y_ref, sfin_ref, st_ref) = refs
    else:
        x_ref, b_ref, c_ref, dt_ref, dtb_ref, alog_ref, s0_ref, y_ref, sfin_ref, st_ref = refs
    ci = pl.program_id(0)

    @pl.when(ci == 0)
    def _():
        st_ref[...] = s0_ref[...]

    n = x_ref.shape[0]
    p, ns = SSM_HEAD_DIM, SSM_STATE
    heads = x_ref.shape[1] // p
    hpg = heads // SSM_GROUPS
    gw = hpg * p
    dt = _softplus(dt_ref[...] + dtb_ref[...])
    a = _cumsum_rows(dt * -jnp.exp(alog_ref[...]), reverse)
    a_t, dt_t = a.T, dt.T
    last = 0 if reverse else n - 1
    a_last = a[last:last + 1, :]
    ti = lax.broadcasted_iota(jnp.int32, (n, n), 0)
    si = lax.broadcasted_iota(jnp.int32, (n, n), 1)
    mask = (si >= ti) if reverse else (si <= ti)
    lane = lax.broadcasted_iota(jnp.int32, (n, LANES), 1)
    lo = lane < p
    lo1 = lax.broadcasted_iota(jnp.int32, (1, LANES), 1) < p
    hoff = heads if reverse else 0
    for g in range(SSM_GROUPS):
        bg = b_ref[:, g * ns:(g + 1) * ns].astype(BF16)
        cg = c_ref[:, g * ns:(g + 1) * ns].astype(BF16)
        cb = _dot_nt(cg, bg)
        cs = _dot(cg, st_ref[:, g * gw:(g + 1) * gw].astype(BF16))
        xw, dec = [], []
        for j in range(hpg // 2):
            h0 = hoff + g * hpg + 2 * j
            sl = slice(g * gw + j * LANES, g * gw + (j + 1) * LANES)
            x2 = x_ref[:, sl]
            x2b = x2.astype(BF16)
            ys = []
            for h in (h0, h0 + 1):
                seg = jnp.where(mask, a[:, h:h + 1] - a_t[h:h + 1, :], NEG)
                w = cb * jnp.exp(seg) * dt_t[h:h + 1, :]
                ys.append(_dot(w.astype(BF16), x2b))
            a_rep = jnp.where(lo, a[:, h0:h0 + 1], a[:, h0 + 1:h0 + 2])
            dt_rep = jnp.where(lo, dt[:, h0:h0 + 1], dt[:, h0 + 1:h0 + 2])
            al_rep = jnp.where(lo1, a_last[:, h0:h0 + 1], a_last[:, h0 + 1:h0 + 2])
            y2 = jnp.where(lo, ys[0], ys[1]) + cs[:, j * LANES:(j + 1) * LANES] * jnp.exp(a_rep)
            xw.append((x2 * (jnp.exp(al_rep - a_rep) * dt_rep)).astype(BF16))
            dec.append(jnp.exp(al_rep))
            if final:
                y2 = y2 + yfw_ref[:, sl] + dsk_ref[:, sl] * x2
                y_ref[:, sl] = y2 * _silu(z_ref[:, sl])
            else:
                y_ref[:, sl] = y2
        gs = slice(g * gw, (g + 1) * gw)
        st_ref[:, gs] = st_ref[:, gs] * jnp.concatenate(dec, 1) + _dot_tn(bg, jnp.concatenate(xw, 1))
        if final:
            yg = y_ref[:, gs]
            y_ref[:, gs] = yg * lax.rsqrt(jnp.mean(yg * yg, -1, keepdims=True) + RMS_EPS) * nw_ref[:, gs]

    @pl.when(ci == pl.num_programs(0) - 1)
    def _():
        sfin_ref[...] = st_ref[...]


def ssd_scan(xbc, zx, dt_col, dt_bias, a_log, s0, reverse, y_fw=None, d_skip=None, norm_w=None):
    n = xbc.shape[0]
    gn = SSM_GROUPS * SSM_STATE
    di = xbc.shape[1] - 2 * gn
    lc = SSM_CHUNK
    nc = n // lc
    final = y_fw is not None
    cidx = (lambda c: nc - 1 - c) if reverse else (lambda c: c)
    wide = lambda col: pl.BlockSpec((lc, di), lambda c: (cidx(c), col))
    vec = lambda w: pl.BlockSpec((1, w), lambda c: (0, 0))
    st = pl.BlockSpec((SSM_STATE, di), lambda c: (0, 0))
    in_specs = [wide(0),
                pl.BlockSpec((lc, gn), lambda c: (cidx(c), di // gn)),
                pl.BlockSpec((lc, gn), lambda c: (cidx(c), di // gn + 1)),
                pl.BlockSpec((lc, LANES), lambda c: (cidx(c), dt_col)),
                vec(LANES), vec(LANES), st]
    args = [xbc, xbc, xbc, zx, dt_bias, a_log, s0]
    if final:
        in_specs += [wide(0), wide(0), vec(di), vec(di)]
        args += [zx, y_fw, d_skip, norm_w]
    return pl.pallas_call(
        functools.partial(_ssd_kernel, reverse=reverse, final=final),
        out_shape=(jax.ShapeDtypeStruct((n, di), F32), jax.ShapeDtypeStruct((SSM_STATE, di), F32)),
        grid=(nc,),
        in_specs=in_specs,
        out_specs=(wide(0), st),
        scratch_shapes=[pltpu.VMEM((SSM_STATE, di), F32)],
        compiler_params=_cp("arbitrary"),
        name="ssd_bw" if reverse else "ssd_fw",
    )(*args)


def _cast_kernel(x_ref, o_ref):
    o_ref[...] = x_ref[...].astype(o_ref.dtype)


def to_bf16(x):
    n, d = x.shape
    tm = min(256, n)
    return pl.pallas_call(
        _cast_kernel,
        out_shape=jax.ShapeDtypeStruct((n, d), BF16),
        grid=(n // tm,),
        in_specs=[pl.BlockSpec((tm, d), lambda i: (i, 0))],
        out_specs=pl.BlockSpec((tm, d), lambda i: (i, 0)),
        compiler_params=_cp("parallel"),
        name="to_bf16",
    )(x)


def _pad_cols(w, n):
    return jnp.pad(w, ((0, 0), (0, n - w.shape[1])))


def swa_mixer(h_ctx, h_lat, w_in, sink, cos2, sin2, with_ctx):
    w = w_in.astype(BF16)
    qkv_l = matmul(h_lat, w, F32, tn=1280)
    qkv_c = matmul(h_ctx, w, F32, tn=1280)
    o_lat, o_ctx = swa_attention(qkv_l, qkv_c, sink.astype(F32), cos2, sin2, with_ctx)
    return o_ctx, o_lat


def hgrn2_mixer(h_ctx, h_lat, w_in, lower_bound, norm_w, with_ctx):
    w = w_in.astype(BF16)
    hk = HG_HEADS * HG_KDIM
    lb = lower_bound.reshape(1, hk)
    nw = norm_w.reshape(1, hk)
    proj_c = matmul(h_ctx, w, F32, tn=1024)
    proj_l = matmul(h_lat, w, F32, tn=1024)
    zeros = jnp.zeros((HG_HEADS, HG_KDIM, HG_KDIM), F32)
    ofw_c, s_fw = hgrn2_scan(proj_c, lb, zeros, False)
    y_c, s_bw = hgrn2_scan(proj_c, lb, zeros, True, ofw_c, nw)
    ofw_l, _ = hgrn2_scan(proj_l, lb, s_fw, False)
    y_l, _ = hgrn2_scan(proj_l, lb, s_bw, True, ofw_l, nw)
    return (y_c if with_ctx else None), y_l


def mla_mixer(h_ctx, h_lat, w_in, q_norm, kv_norm, w_uq, w_ukv, cos2, sin2, with_ctx):
    t, nctx = h_lat.shape[0], h_ctx.shape[0]
    hh = MLA_HEADS
    ncol = MLA_Q_RANK + MLA_KV_RANK + LANES
    w1 = _pad_cols(w_in, ncol).astype(BF16)
    wq = w_uq.reshape(MLA_Q_RANK, hh, MLA_NOPE + MLA_ROPE)
    wq = jnp.pad(wq, ((0, 0), (0, 0), (0, LANES - MLA_ROPE))).reshape(MLA_Q_RANK, hh * (MLA_NOPE + LANES)).astype(BF16)
    wkv = w_ukv.reshape(MLA_KV_RANK, hh, MLA_NOPE + MLA_V)
    wkv = jnp.concatenate([wkv[:, :, :MLA_NOPE].reshape(MLA_KV_RANK, hh * MLA_NOPE),
                           wkv[:, :, MLA_NOPE:].reshape(MLA_KV_RANK, hh * MLA_V)], 1).astype(BF16)
    zpad = jnp.zeros((t, LANES - 2 * 32), F32)
    cos_l = jnp.concatenate([cos2, zpad], 1)
    sin_l = jnp.concatenate([sin2, zpad], 1)
    cos_c = jnp.ones((nctx, LANES), F32)
    sin_c = jnp.zeros((nctx, LANES), F32)
    qn, kvn = q_norm.reshape(1, -1), kv_norm.reshape(1, -1)

    def proj(h, cos, sin):
        cq, ckv, kpe = mla_norm(matmul(h, w1, F32, tn=ncol), qn, kvn, cos, sin)
        return matmul(cq, wq, F32, tn=1024), matmul(ckv, wkv, BF16, tn=1024), kpe

    q_c, kv_c, kpe_c = proj(h_ctx, cos_c, sin_c)
    q_l, kv_l, kpe_l = proj(h_lat, cos_l, sin_l)
    o_ctx = mla_attention(q_c, cos_c, sin_c, kv_c, kpe_c, tq=nctx, tk=nctx) if with_ctx else None
    kv_all = jnp.concatenate([kv_l, kv_c], 0)
    kpe_all = jnp.concatenate([kpe_l, kpe_c], 0)
    o_lat = mla_attention(q_l, cos_l, sin_l, kv_all, kpe_all, tq=1024, tk=1280)
    return o_ctx, o_lat


def mamba2_mixer(h_ctx, h_lat, w_in, conv_w, conv_b, dt_bias, a_log, d_skip, norm_w, with_ctx):
    di = norm_w.shape[0]
    gn = SSM_GROUPS * SSM_STATE
    heads = di // SSM_HEAD_DIM
    ncv = di + 2 * gn
    w = w_in.astype(BF16)
    dtb = dt_bias.reshape(1, 2 * heads).astype(F32)
    alog = a_log.reshape(1, 2 * heads).astype(F32)
    dsk = jnp.repeat(d_skip.astype(F32), SSM_HEAD_DIM).reshape(1, di)
    nw = norm_w.reshape(1, di)
    dt_col = (di + ncv) // LANES

    def prep(h):
        zx = matmul(h, w, F32, tn=1152)
        return zx, dwconv_silu(zx, conv_w, conv_b, di, ncv)

    zx_c, xbc_c = prep(h_ctx)
    zx_l, xbc_l = prep(h_lat)
    zeros = jnp.zeros((SSM_STATE, di), F32)
    yfw_c, s_fw = ssd_scan(xbc_c, zx_c, dt_col, dtb, alog, zeros, False)
    y_c, s_bw = ssd_scan(xbc_c, zx_c, dt_col, dtb, alog, zeros, True, yfw_c, dsk, nw)
    yfw_l, _ = ssd_scan(xbc_l, zx_l, dt_col, dtb, alog, s_fw, False)
    y_l, _ = ssd_scan(xbc_l, zx_l, dt_col, dtb, alog, s_bw, True, yfw_l, dsk, nw)
    return (to_bf16(y_c) if with_ctx else None), to_bf16(y_l)


def kernel(x, c, ctx, c_ctx, ada_w, ada_b, ln_g, ln_b, ffn_up, ffn_conv_w, ffn_conv_b, ffn_down, swa_in, swa_sink, swa_out, hg_in, hg_lb_logits, hg_norm_w, hg_out, mla_in, mla_q_norm, mla_kv_norm, mla_uq, mla_ukv, mla_out, ssm_in, ssm_conv_w, ssm_conv_b, ssm_dt_bias, ssm_A_log, ssm_D, ssm_norm_w, ssm_out):
    batch, t, d = x.shape
    assert batch == 1
    depth = ada_w.shape[0]
    x_lat, x_ctx = x[0], ctx[0]

    probs = jax.nn.softmax(hg_lb_logits.astype(F32), axis=0)
    lower_bounds = jnp.cumsum(probs, axis=0) - probs[0]

    cvec = jnp.zeros((8, d), F32).at[0].set(c[0]).at[1].set(c_ctx)
    mods = ada_modulation(cvec, ada_w, ada_b)

    def mod(i, stream, j):
        return mods[i, stream:stream + 1, j * d:(j + 1) * d]

    cos2, sin2 = rope_tables(t)
    zero = jnp.zeros((1, d), F32)
    h_lat = modulate(x_lat, mod(0, 0, 0), mod(0, 0, 1))
    h_ctx = modulate(x_ctx, mod(0, 1, 0), mod(0, 1, 1))
    for i in range(depth):
        kind, j = i % N_MIXERS, i // N_MIXERS
        with_ctx = i < depth - 1
        if kind == 0:
            y_ctx, y_lat = swa_mixer(h_ctx, h_lat, swa_in[j], swa_sink[j], cos2, sin2, with_ctx)
            w_out = swa_out[j]
        elif kind == 1:
            y_ctx, y_lat = hgrn2_mixer(h_ctx, h_lat, hg_in[j], lower_bounds[i], hg_norm_w[j], with_ctx)
            w_out = hg_out[j]
        elif kind == 2:
            y_ctx, y_lat = mla_mixer(h_ctx, h_lat, mla_in[j], mla_q_norm[j], mla_kv_norm[j], mla_uq[j], mla_ukv[j],
                                     cos2, sin2, with_ctx)
            w_out = mla_out[j]
        else:
            y_ctx, y_lat = mamba2_mixer(h_ctx, h_lat, ssm_in[j], ssm_conv_w[j], ssm_conv_b[j], ssm_dt_bias[j],
                                        ssm_A_log[j], ssm_D[j], ssm_norm_w[j], with_ctx)
            w_out = ssm_out[j]
        w_out = w_out.astype(BF16)
        w_up, w_down = ffn_up[i].astype(BF16), ffn_down[i].astype(BF16)
        g0, b0, g1, b1 = ln_g[i, 0:1], ln_b[i, 0:1], ln_g[i, 1:2], ln_b[i, 1:2]
        streams = [(0, x_lat, y_lat)] + ([(1, x_ctx, y_ctx)] if with_ctx else [])
        outs = []
        for s, xs, ys in streams:
            xs, hf = matmul_ln(ys, w_out, xs, mod(i, s, 2), g0, b0, mod(i, s, 3), mod(i, s, 4))
            nsh, nsc = (mod(i + 1, s, 0), mod(i + 1, s, 1)) if i + 1 < depth else (zero, zero)
            outs.append(conv_ffn_ln(hf, w_up, ffn_conv_w[i], ffn_conv_b[i], w_down, xs, mod(i, s, 5), g1, b1, nsh, nsc))
        x_lat, h_lat = outs[0]
        if with_ctx:
            x_ctx, h_ctx = outs[1]
    return x_lat[None]
```

```python
import functools

import jax
import jax.numpy as jnp
from jax import lax
from jax.experimental import pallas as pl
from jax.experimental.pallas import tpu as pltpu

F32, BF16 = jnp.float32, jnp.bfloat16

DEPTH, N_MIXERS = 4, 4
GRID_W = 64
ALPHA = (2.0 * DEPTH) ** 0.25
LN_EPS, RMS_EPS, ROPE_BASE = 1e-5, 1e-6, 10000.0
SWA_HEADS, SWA_KV_HEADS, SWA_HEAD_DIM, SWA_BLOCK = 32, 4, 64, 128
HG_HEADS, HG_KDIM = 16, 128
MLA_HEADS, MLA_Q_RANK, MLA_KV_RANK, MLA_NOPE, MLA_ROPE, MLA_V = 16, 512, 512, 128, 64, 128
SSM_HEAD_DIM, SSM_STATE, SSM_GROUPS, SSM_CHUNK = 64, 128, 8, 128
HG_CHUNK = 64
LANES = 128
NEG = -1e30
LOG2E = 1.4426950408889634
VMEM_LIMIT = 56 * 1024 * 1024


def _cp(*sem):
    return pltpu.CompilerParams(dimension_semantics=sem, vmem_limit_bytes=VMEM_LIMIT)


def _dot(a, b):
    return jnp.dot(a, b, preferred_element_type=F32)


def _dot_nt(a, b):
    return lax.dot_general(a, b, (((1,), (1,)), ((), ())), preferred_element_type=F32)


def _dot_tn(a, b):
    return lax.dot_general(a, b, (((0,), (0,)), ((), ())), preferred_element_type=F32)


def _sigmoid(x):
    return 1.0 / (1.0 + jnp.exp(-x))


def _silu(x):
    return x * _sigmoid(x)


def _softplus(x):
    return jnp.maximum(x, 0.0) + jnp.log(1.0 + jnp.exp(-jnp.abs(x)))


def _ln_mod(z, g, b):
    mu = jnp.mean(z, -1, keepdims=True)
    d = z - mu
    var = jnp.mean(d * d, -1, keepdims=True)
    return d * lax.rsqrt(var + LN_EPS) * g + b


def _shift_rows(x, prev_row, next_row):
    n = x.shape[0]
    row = lax.broadcasted_iota(jnp.int32, (n, 1), 0)
    dn = jnp.where(row == 0, prev_row, pltpu.roll(x, 1, 0))
    up = jnp.where(row == n - 1, next_row, pltpu.roll(x, n - 1, 0))
    return dn, up


def _cumsum_rows(x, reverse):
    n = x.shape[0]
    row = lax.broadcasted_iota(jnp.int32, (n, 1), 0)
    s = 1
    while s < n:
        if reverse:
            x = x + jnp.where(row < n - s, pltpu.roll(x, n - s, 0), 0.0)
        else:
            x = x + jnp.where(row >= s, pltpu.roll(x, s, 0), 0.0)
        s *= 2
    return x


def _ada_kernel(c_ref, w_ref, b_ref, o_ref):
    a = _silu(c_ref[...]).astype(BF16)
    o_ref[0] = _dot(a, w_ref[0].astype(BF16)) + b_ref[0]


def ada_modulation(cvec, ada_w, ada_b):
    depth, d, n = ada_w.shape
    tn = 1024
    return pl.pallas_call(
        _ada_kernel,
        out_shape=jax.ShapeDtypeStruct((depth, 8, n), F32),
        grid=(depth, n // tn),
        in_specs=[pl.BlockSpec((8, d), lambda i, j: (0, 0)),
                  pl.BlockSpec((1, d, tn), lambda i, j: (i, 0, j)),
                  pl.BlockSpec((1, 1, tn), lambda i, j: (i, 0, j))],
        out_specs=pl.BlockSpec((1, 8, tn), lambda i, j: (i, 0, j)),
        compiler_params=_cp("parallel", "parallel"),
        name="ada_modulation",
    )(cvec, ada_w, ada_b.reshape(depth, 1, n))


def _modulate_kernel(x_ref, sh_ref, sc_ref, o_ref):
    o_ref[...] = (x_ref[...] * (1.0 + sc_ref[...]) + sh_ref[...]).astype(o_ref.dtype)


def modulate(x, shift, scale):
    m, d = x.shape
    tm = min(512, m)
    vec = pl.BlockSpec((1, d), lambda i: (0, 0))
    return pl.pallas_call(
        _modulate_kernel,
        out_shape=jax.ShapeDtypeStruct((m, d), BF16),
        grid=(m // tm,),
        in_specs=[pl.BlockSpec((tm, d), lambda i: (i, 0)), vec, vec],
        out_specs=pl.BlockSpec((tm, d), lambda i: (i, 0)),
        compiler_params=_cp("parallel"),
        name="modulate",
    )(x, shift, scale)


def _mm_kernel(a_ref, w_ref, o_ref):
    o_ref[...] = _dot(a_ref[...], w_ref[...]).astype(o_ref.dtype)


def matmul(a, w, out_dtype, tn, tm=1024):
    m, k = a.shape
    n = w.shape[1]
    tm = min(tm, m)
    assert m % tm == 0 and n % tn == 0
    return pl.pallas_call(
        _mm_kernel,
        out_shape=jax.ShapeDtypeStruct((m, n), out_dtype),
        grid=(m // tm, n // tn),
        in_specs=[pl.BlockSpec((tm, k), lambda i, j: (i, 0)),
                  pl.BlockSpec((k, tn), lambda i, j: (0, j))],
        out_specs=pl.BlockSpec((tm, tn), lambda i, j: (i, j)),
        compiler_params=_cp("parallel", "parallel"),
        name="matmul",
    )(a, w)


def _mm_ln_kernel(a_ref, w_ref, x_ref, gate_ref, g_ref, b_ref, sh_ref, sc_ref, xo_ref, ho_ref, acc_ref):
    k = pl.program_id(1)

    @pl.when(k == 0)
    def _():
        acc_ref[...] = jnp.zeros_like(acc_ref)

    acc_ref[...] += _dot(a_ref[...], w_ref[...])

    @pl.when(k == pl.num_programs(1) - 1)
    def _():
        xn = _ln_mod(ALPHA * x_ref[...] + gate_ref[...] * acc_ref[...], g_ref[...], b_ref[...])
        xo_ref[...] = xn
        ho_ref[...] = (xn * (1.0 + sc_ref[...]) + sh_ref[...]).astype(ho_ref.dtype)


def matmul_ln(a, w, x, gate, g, b, shift, scale):
    m, k = a.shape
    d = w.shape[1]
    tm, tk = min(512, m), 2048
    vec = pl.BlockSpec((1, d), lambda i, kk: (0, 0))
    row = pl.BlockSpec((tm, d), lambda i, kk: (i, 0))
    return pl.pallas_call(
        _mm_ln_kernel,
        out_shape=(jax.ShapeDtypeStruct((m, d), F32), jax.ShapeDtypeStruct((m, d), BF16)),
        grid=(m // tm, k // tk),
        in_specs=[pl.BlockSpec((tm, tk), lambda i, kk: (i, kk)),
                  pl.BlockSpec((tk, d), lambda i, kk: (kk, 0)),
                  row, vec, vec, vec, vec, vec],
        out_specs=(row, row),
        scratch_shapes=[pltpu.VMEM((tm, d), F32)],
        compiler_params=_cp("parallel", "arbitrary"),
        name="matmul_ln",
    )(a, w, x, gate, g, b, shift, scale)


def _ffn_kernel(h_ref, hp_ref, hn_ref, wa_ref, wv_ref, cw_ref, cb_ref, wd_ref, x_ref, gate_ref, g_ref, b_ref,
                sh_ref, sc_ref, xo_ref, ho_ref, acc_ref):
    i, f = pl.program_id(0), pl.program_id(1)

    @pl.when(f == 0)
    def _():
        acc_ref[...] = jnp.zeros_like(acc_ref)

    wa = wa_ref[...]
    nrow = hp_ref.shape[0]
    a_prev = _dot(hp_ref[...], wa)[nrow - 1:nrow, :]
    a_next = _dot(hn_ref[...], wa)[0:1, :]
    a_prev = jnp.where(i > 0, a_prev, 0.0)
    a_next = jnp.where(i < pl.num_programs(0) - 1, a_next, 0.0)
    cw = cw_ref[...]
    a = _dot(h_ref[...], wa)
    v = _dot(h_ref[...], wv_ref[...])
    dn, up = _shift_rows(a, a_prev, a_next)
    cv = cw[0:1, :] * dn + cw[1:2, :] * a + cw[2:3, :] * up + cb_ref[...]
    u = (_silu(cv) * v).astype(BF16)
    acc_ref[...] += _dot(u, wd_ref[...])

    @pl.when(f == pl.num_programs(1) - 1)
    def _():
        xn = _ln_mod(ALPHA * x_ref[...] + gate_ref[...] * acc_ref[...], g_ref[...], b_ref[...])
        xo_ref[...] = xn
        ho_ref[...] = (xn * (1.0 + sc_ref[...]) + sh_ref[...]).astype(ho_ref.dtype)


def conv_ffn_ln(h, w_up, conv_w, conv_b, w_down, x, gate, g, b, shift, scale):
    m, d = h.shape
    f = w_down.shape[0]
    tm, tf, halo = min(512, m), 512, 16
    nf = f // tf
    nh = m // halo
    vec = pl.BlockSpec((1, d), lambda i, j: (0, 0))
    row = pl.BlockSpec((tm, d), lambda i, j: (i, 0))
    return pl.pallas_call(
        _ffn_kernel,
        out_shape=(jax.ShapeDtypeStruct((m, d), F32), jax.ShapeDtypeStruct((m, d), BF16)),
        grid=(m // tm, nf),
        in_specs=[row,
                  pl.BlockSpec((halo, d), lambda i, j: (jnp.maximum(i * (tm // halo) - 1, 0), 0)),
                  pl.BlockSpec((halo, d), lambda i, j: (jnp.minimum((i + 1) * (tm // halo), nh - 1), 0)),
                  pl.BlockSpec((d, tf), lambda i, j: (0, j)),
                  pl.BlockSpec((d, tf), lambda i, j: (0, nf + j)),
                  pl.BlockSpec((3, tf), lambda i, j: (0, j)),
                  pl.BlockSpec((1, tf), lambda i, j: (0, j)),
                  pl.BlockSpec((tf, d), lambda i, j: (j, 0)),
                  row, vec, vec, vec, vec, vec],
        out_specs=(row, row),
        scratch_shapes=[pltpu.VMEM((tm, d), F32)],
        compiler_params=_cp("parallel", "arbitrary"),
        name="conv_ffn_ln",
    )(h, h, h, w_up, w_up, conv_w, conv_b.reshape(1, f), w_down, x, gate, g, b, shift, scale)


def rope_tables(n_tokens):
    rows = n_tokens // GRID_W
    row = jnp.repeat(jnp.arange(rows, dtype=F32), GRID_W)
    col = jnp.tile(jnp.arange(GRID_W, dtype=F32), rows)
    n_axis = 16
    inv_freq = ROPE_BASE ** (-jnp.arange(n_axis, dtype=F32) / n_axis)
    ang = jnp.concatenate([row[:, None] * inv_freq, col[:, None] * inv_freq], -1)
    cos, sin = jnp.cos(ang), jnp.sin(ang)
    return jnp.concatenate([cos, cos], -1), jnp.concatenate([-sin, sin], -1)


def _rope2(x, cos, sin):
    lane = lax.broadcasted_iota(jnp.int32, x.shape, 1)
    rot = jnp.where((lane & 63) < 32, pltpu.roll(x, 96, 1), pltpu.roll(x, 32, 1))
    return x * cos + rot * sin


def _split_halves(x, upper):
    lane = lax.broadcasted_iota(jnp.int32, x.shape, 1)
    if upper:
        hi = jnp.where(lane >= 64, x, 0.0)
        return pltpu.roll(hi, 64, 1), hi
    lo = jnp.where(lane < 64, x, 0.0)
    return lo, pltpu.roll(lo, 64, 1)


def _block_diag(pieces, upper):
    halves = [_split_halves(x, upper) for x in pieces]
    return jnp.concatenate([h[0] for h in halves] + [h[1] for h in halves], 0).astype(BF16)


def _swa_groups(sink_ref, q_ref, keys, vals, bias, o_ref, rope):
    blk = q_ref.shape[0]
    hpg = SWA_HEADS // SWA_KV_HEADS // 2
    rowp = lax.shift_right_logical(lax.broadcasted_iota(jnp.int32, (hpg * blk, 1), 0), blk.bit_length() - 1)
    lane = lax.broadcasted_iota(jnp.int32, (hpg * blk, LANES), 1)
    for g in range(SWA_KV_HEADS):
        qs = []
        for pp in range(hpg):
            q2 = q_ref[:, (g * hpg + pp) * LANES:(g * hpg + pp + 1) * LANES]
            if rope is not None:
                q2 = _rope2(q2, *rope)
            qs.append(q2 * (SWA_HEAD_DIM ** -0.5 * LOG2E))
        s = _dot_nt(jnp.concatenate(qs, 0).astype(BF16), keys[g])
        nk = s.shape[1] // 2
        ps, dens = [], []
        for half in range(2):
            sh = s[:, half * nk:(half + 1) * nk]
            if bias is not None:
                sh = sh + bias
            sink = jnp.zeros((hpg * blk, 1), F32)
            for pp in range(hpg):
                sink = jnp.where(rowp == pp, sink_ref[2 * (g * hpg + pp) + half] * LOG2E, sink)
            m = jnp.maximum(jnp.max(sh, -1, keepdims=True), sink)
            p = jnp.exp2(sh - m)
            dens.append(jnp.sum(p, -1, keepdims=True) + jnp.exp2(sink - m))
            ps.append(p.astype(BF16))
        o = _dot(jnp.concatenate(ps, 1), vals[g]) / jnp.where(lane < 64, dens[0], dens[1])
        for pp in range(hpg):
            o_ref[:, (g * hpg + pp) * LANES:(g * hpg + pp + 1) * LANES] = o[pp * blk:(pp + 1) * blk].astype(o_ref.dtype)


def _swa_lat_kernel(sink_ref, q_ref, kp_ref, kc_ref, kn_ref, vp_ref, vc_ref, vn_ref, kx_ref, vx_ref,
                    cq_ref, sq_ref, cp_ref, sp_ref, cn_ref, sn_ref, o_ref):
    i, nb = pl.program_id(0), pl.num_programs(0)
    blk = q_ref.shape[0]
    nctx = kx_ref.shape[0]
    keys, vals = [], []
    for g in range(SWA_KV_HEADS):
        sl = slice((g // 2) * LANES, (g // 2 + 1) * LANES)
        up = g % 2 == 1
        keys.append(_block_diag([_rope2(kp_ref[:, sl], cp_ref[...], sp_ref[...]),
                                 _rope2(kc_ref[:, sl], cq_ref[...], sq_ref[...]),
                                 _rope2(kn_ref[:, sl], cn_ref[...], sn_ref[...]), kx_ref[:, sl]], up))
        vals.append(_block_diag([vp_ref[:, sl], vc_ref[:, sl], vn_ref[:, sl], vx_ref[:, sl]], up))
    nk = 3 * blk + nctx
    rows = (SWA_HEADS // SWA_KV_HEADS // 2) * blk
    r = lax.broadcasted_iota(jnp.int32, (rows, nk), 0) & (blk - 1)
    c = lax.broadcasted_iota(jnp.int32, (rows, nk), 1)
    has_prev = jnp.where(i > 0, 0.0, NEG)
    has_next = jnp.where(i < nb - 1, 0.0, NEG)
    bias = jnp.where(c < blk, jnp.where(c >= r, has_prev, NEG),
                     jnp.where(c < 2 * blk, 0.0,
                               jnp.where(c < 3 * blk, jnp.where(c - 2 * blk <= r, has_next, NEG), 0.0)))
    _swa_groups(sink_ref, q_ref, keys, vals, bias, o_ref, (cq_ref[...], sq_ref[...]))


def _swa_ctx_kernel(sink_ref, q_ref, kx_ref, vx_ref, o_ref):
    keys, vals = [], []
    for g in range(SWA_KV_HEADS):
        sl = slice((g // 2) * LANES, (g // 2 + 1) * LANES)
        keys.append(_block_diag([kx_ref[:, sl]], g % 2 == 1))
        vals.append(_block_diag([vx_ref[:, sl]], g % 2 == 1))
    _swa_groups(sink_ref, q_ref, keys, vals, None, o_ref, None)


def swa_attention(qkv_lat, qkv_ctx, sink, cos2, sin2, with_ctx):
    t, nctx = qkv_lat.shape[0], qkv_ctx.shape[0]
    blk, dq = SWA_BLOCK, SWA_HEADS * SWA_HEAD_DIM
    dkv = SWA_KV_HEADS * SWA_HEAD_DIM
    kcol, vcol = dq // dkv, dq // dkv + 1
    nb = t // blk
    cos = jnp.concatenate([cos2, cos2], -1)
    sin = jnp.concatenate([sin2, sin2], -1)
    smem = pl.BlockSpec(memory_space=pltpu.SMEM)
    prev = lambda i: jnp.maximum(i - 1, 0)
    nxt = lambda i: jnp.minimum(i + 1, nb - 1)
    tab = lambda f: pl.BlockSpec((blk, LANES), lambda i: (f(i), 0))
    kv = lambda f, col: pl.BlockSpec((blk, dkv), lambda i: (f(i), col))
    same = lambda i: i
    o_lat = pl.pallas_call(
        _swa_lat_kernel,
        out_shape=jax.ShapeDtypeStruct((t, dq), BF16),
        grid=(nb,),
        in_specs=[smem, pl.BlockSpec((blk, dq), lambda i: (i, 0)),
                  kv(prev, kcol), kv(same, kcol), kv(nxt, kcol), kv(prev, vcol), kv(same, vcol), kv(nxt, vcol),
                  pl.BlockSpec((nctx, dkv), lambda i: (0, kcol)), pl.BlockSpec((nctx, dkv), lambda i: (0, vcol)),
                  tab(same), tab(same), tab(prev), tab(prev), tab(nxt), tab(nxt)],
        out_specs=pl.BlockSpec((blk, dq), lambda i: (i, 0)),
        compiler_params=_cp("parallel"),
        name="swa_lat",
    )(sink, qkv_lat, qkv_lat, qkv_lat, qkv_lat, qkv_lat, qkv_lat, qkv_lat, qkv_ctx, qkv_ctx,
      cos, sin, cos, sin, cos, sin)
    if not with_ctx:
        return o_lat, None
    o_ctx = pl.pallas_call(
        _swa_ctx_kernel,
        out_shape=jax.ShapeDtypeStruct((nctx, dq), BF16),
        grid=(nctx // blk,),
        in_specs=[smem, pl.BlockSpec((blk, dq), lambda i: (i, 0)),
                  pl.BlockSpec((nctx, dkv), lambda i: (0, kcol)), pl.BlockSpec((nctx, dkv), lambda i: (0, vcol))],
        out_specs=pl.BlockSpec((blk, dq), lambda i: (i, 0)),
        compiler_params=_cp("parallel"),
        name="swa_ctx",
    )(sink, qkv_ctx, qkv_ctx, qkv_ctx)
    return o_lat, o_ctx


def _hg_kernel(*refs, reverse, final):
    if final:
        q_ref, f_ref, v_ref, lb_ref, s0_ref, g_ref, ofw_ref, nw_ref, o_ref, sfin_ref, st_ref = refs
    else:
        q_ref, f_ref, v_ref, lb_ref, s0_ref, o_ref, sfin_ref, st_ref = refs
    c = pl.program_id(0)

    @pl.when(c == 0)
    def _():
        st_ref[...] = s0_ref[...]

    n = q_ref.shape[0]
    kd = HG_KDIM
    lb = lb_ref[...]
    q = _silu(q_ref[...]) * kd ** -0.5
    sg = _sigmoid(f_ref[...])
    lf = jnp.log(lb + (1.0 - lb) * sg)
    k = (1.0 - lb) * (1.0 - sg)
    v = v_ref[...].astype(BF16)
    b = _cumsum_rows(lf, reverse)
    mid = n // 2 if reverse else n // 2 - 1
    last = 0 if reverse else n - 1
    r = b[mid:mid + 1, :]
    b_last = b[last:last + 1, :]
    qe = (q * jnp.exp(b - r)).astype(BF16)
    ke = (k * jnp.exp(r - b)).astype(BF16)
    qs = (q * jnp.exp(b)).astype(BF16)
    kl = (k * jnp.exp(b_last - b)).astype(BF16)
    dec = jnp.exp(b_last)
    ti = lax.broadcasted_iota(jnp.int32, (n, n), 0)
    si = lax.broadcasted_iota(jnp.int32, (n, n), 1)
    mask = (si >= ti) if reverse else (si <= ti)
    for h in range(q_ref.shape[1] // kd):
        sl = slice(h * kd, (h + 1) * kd)
        att = jnp.where(mask, _dot_nt(qe[:, sl], ke[:, sl]), 0.0).astype(BF16)
        st = st_ref[h]
        o = _dot(att, v[:, sl]) + _dot_nt(qs[:, sl], st.astype(BF16))
        st_ref[h] = st * dec[:, sl] + _dot_tn(v[:, sl], kl[:, sl])
        if final:
            o = o + ofw_ref[:, sl]
            o = o * lax.rsqrt(jnp.mean(o * o, -1, keepdims=True) + RMS_EPS) * nw_ref[:, sl]
            o_ref[:, sl] = (o * _silu(g_ref[:, sl])).astype(o_ref.dtype)
        else:
            o_ref[:, sl] = o

    @pl.when(c == pl.num_programs(0) - 1)
    def _():
        sfin_ref[...] = st_ref[...]


def hgrn2_scan(proj, lb, s0, reverse, o_fw=None, norm_w=None):
    n = proj.shape[0]
    hk = HG_HEADS * HG_KDIM
    lc = HG_CHUNK
    nc = n // lc
    final = o_fw is not None
    cidx = (lambda c: nc - 1 - c) if reverse else (lambda c: c)
    blk = lambda col: pl.BlockSpec((lc, hk), lambda c: (cidx(c), col))
    vec = pl.BlockSpec((1, hk), lambda c: (0, 0))
    st = pl.BlockSpec((HG_HEADS, HG_KDIM, HG_KDIM), lambda c: (0, 0, 0))
    in_specs = [blk(0), blk(2 if reverse else 1), blk(3), vec, st]
    args = [proj, proj, proj, lb, s0]
    if final:
        in_specs += [blk(4), blk(0), vec]
        args += [proj, o_fw, norm_w]
    return pl.pallas_call(
        functools.partial(_hg_kernel, reverse=reverse, final=final),
        out_shape=(jax.ShapeDtypeStruct((n, hk), BF16 if final else F32),
                   jax.ShapeDtypeStruct((HG_HEADS, HG_KDIM, HG_KDIM), F32)),
        grid=(nc,),
        in_specs=in_specs,
        out_specs=(blk(0), st),
        scratch_shapes=[pltpu.VMEM((HG_HEADS, HG_KDIM, HG_KDIM), F32)],
        compiler_params=_cp("arbitrary"),
        name="hgrn2_bw" if reverse else "hgrn2_fw",
    )(*args)


def _mla_norm_kernel(p_ref, qn_ref, kvn_ref, cos_ref, sin_ref, cq_ref, ckv_ref, kpe_ref):
    cq = p_ref[:, :MLA_Q_RANK]
    ckv = p_ref[:, MLA_Q_RANK:MLA_Q_RANK + MLA_KV_RANK]
    kp = p_ref[:, MLA_Q_RANK + MLA_KV_RANK:]
    cq_ref[...] = (cq * lax.rsqrt(jnp.mean(cq * cq, -1, keepdims=True) + RMS_EPS) * qn_ref[...]).astype(BF16)
    ckv_ref[...] = (ckv * lax.rsqrt(jnp.mean(ckv * ckv, -1, keepdims=True) + RMS_EPS) * kvn_ref[...]).astype(BF16)
    rot = pltpu.roll(kp, 96, 1) + pltpu.roll(kp, 32, 1)
    kpe_ref[...] = (kp * cos_ref[...] + rot * sin_ref[...]).astype(BF16)


def mla_norm(proj, q_norm, kv_norm, cos, sin):
    m = proj.shape[0]
    tm = 640 if m % 640 == 0 else 128
    assert m % tm == 0
    return pl.pallas_call(
        _mla_norm_kernel,
        out_shape=(jax.ShapeDtypeStruct((m, MLA_Q_RANK), BF16), jax.ShapeDtypeStruct((m, MLA_KV_RANK), BF16),
                   jax.ShapeDtypeStruct((m, LANES), BF16)),
        grid=(m // tm,),
        in_specs=[pl.BlockSpec((tm, proj.shape[1]), lambda i: (i, 0)),
                  pl.BlockSpec((1, MLA_Q_RANK), lambda i: (0, 0)), pl.BlockSpec((1, MLA_KV_RANK), lambda i: (0, 0)),
                  pl.BlockSpec((tm, LANES), lambda i: (i, 0)), pl.BlockSpec((tm, LANES), lambda i: (i, 0))],
        out_specs=(pl.BlockSpec((tm, MLA_Q_RANK), lambda i: (i, 0)), pl.BlockSpec((tm, MLA_KV_RANK), lambda i: (i, 0)),
                   pl.BlockSpec((tm, LANES), lambda i: (i, 0))),
        compiler_params=_cp("parallel"),
        name="mla_norm",
    )(proj, q_norm, kv_norm, cos, sin)


def _mla_attn_kernel(q_ref, cos_ref, sin_ref, kn_ref, kpe_ref, v_ref, o_ref,
                     q_scr, s0_scr, s1_scr, p0_scr, p1_scr, a0_scr, a1_scr, m_scr, acc_scr, *, tk):
    n = kn_ref.shape[0] // tk
    s_scr, p_scr, a_scr = (s0_scr, s1_scr), (p0_scr, p1_scr), (a0_scr, a1_scr)

    qp = q_ref[:, MLA_NOPE:]
    rot = pltpu.roll(qp, 96, 1) + pltpu.roll(qp, 32, 1)
    qp = qp * cos_ref[...] + rot * sin_ref[...]
    scale = (MLA_NOPE + MLA_ROPE) ** -0.5 * LOG2E
    q_scr[...] = (jnp.concatenate([q_ref[:, :MLA_NOPE], qp], 1) * scale).astype(BF16)
    m_scr[...] = jnp.full_like(m_scr, NEG)
    acc_scr[...] = jnp.zeros_like(acc_scr)
    ones = jnp.ones((tk, LANES), BF16)

    def rows(j):
        return pl.ds(j * tk, tk) if isinstance(j, int) else pl.ds(pl.multiple_of(j * tk, tk), tk)

    def scores(j, slot):
        k = jnp.concatenate([kn_ref[rows(j), :], kpe_ref[rows(j), :]], 1)
        s_scr[slot][...] = _dot_nt(q_scr[...], k)

    def softmax(slot):
        s = s_scr[slot][...]
        m_old = m_scr[...]
        m_new = jnp.maximum(m_old, jnp.max(s, -1, keepdims=True))
        p_scr[slot][...] = jnp.exp2(s - m_new).astype(BF16)
        a_scr[slot][...] = jnp.exp2(m_old - m_new)
        m_scr[...] = m_new

    def weighted(j, slot):
        v1 = jnp.concatenate([v_ref[rows(j), :], ones], 1)
        acc_scr[...] = a_scr[slot][...] * acc_scr[...] + _dot(p_scr[slot][...], v1)

    def steady(t, a, b):
        scores(t + 1, a)
        softmax(b)
        weighted(t - 1, a)

    scores(0, 0)
    if n > 1:
        scores(1, 1)
        softmax(0)
        pairs = (n - 2) // 2

        def pair(i, carry):
            steady(1 + 2 * i, 0, 1)
            steady(2 + 2 * i, 1, 0)
            return carry

        if pairs > 0:
            lax.fori_loop(0, pairs, pair, 0)
        if (n - 2) % 2:
            steady(n - 2, 0, 1)
        weighted(n - 2, (n - 2) % 2)
    softmax((n - 1) % 2)
    weighted(n - 1, (n - 1) % 2)
    o_ref[...] = (acc_scr[:, :MLA_V] / acc_scr[:, MLA_V:]).astype(o_ref.dtype)


def mla_attention(q, cos, sin, kv, kpe, q_rows, k_rows, tq, tk):
    (q0, nq), (k0, nk) = q_rows, k_rows
    hq = MLA_NOPE + LANES
    assert nk % tk == 0 and nq % tq == 0 and q0 % tq == 0 and k0 % nk == 0
    qb, kb = q0 // tq, k0 // nk
    return pl.pallas_call(
        functools.partial(_mla_attn_kernel, tk=tk),
        out_shape=jax.ShapeDtypeStruct((nq, MLA_HEADS * MLA_V), BF16),
        grid=(MLA_HEADS, nq // tq),
        in_specs=[pl.BlockSpec((tq, hq), lambda h, i: (qb + i, h)),
                  pl.BlockSpec((tq, LANES), lambda h, i: (qb + i, 0)),
                  pl.BlockSpec((tq, LANES), lambda h, i: (qb + i, 0)),
                  pl.BlockSpec((nk, MLA_NOPE), lambda h, i: (kb, h)),
                  pl.BlockSpec((nk, LANES), lambda h, i: (kb, 0)),
                  pl.BlockSpec((nk, MLA_V), lambda h, i: (kb, MLA_HEADS + h))],
        out_specs=pl.BlockSpec((tq, MLA_V), lambda h, i: (i, h)),
        scratch_shapes=[pltpu.VMEM((tq, hq), BF16),
                        pltpu.VMEM((tq, tk), F32), pltpu.VMEM((tq, tk), F32),
                        pltpu.VMEM((tq, tk), BF16), pltpu.VMEM((tq, tk), BF16),
                        pltpu.VMEM((tq, 1), F32), pltpu.VMEM((tq, 1), F32), pltpu.VMEM((tq, 1), F32),
                        pltpu.VMEM((tq, MLA_V + LANES), F32)],
        compiler_params=_cp("parallel", "parallel"),
        name="mla_attention",
    )(q, cos, sin, kv, kpe, kv)


def _mm_conv_silu_kernel(h_ref, hp_ref, hn_ref, w_ref, cw_ref, cb_ref, o_ref):
    i = pl.program_id(0)
    w = w_ref[...]
    nrow = hp_ref.shape[0]
    a = _dot(h_ref[...], w)
    a_prev = jnp.where(i > 0, _dot(hp_ref[...], w)[nrow - 1:nrow, :], 0.0)
    a_next = jnp.where(i < pl.num_programs(0) - 1, _dot(hn_ref[...], w)[0:1, :], 0.0)
    dn, up = _shift_rows(a, a_prev, a_next)
    cw = cw_ref[...]
    o_ref[...] = _silu(cw[0:1, :] * dn + cw[1:2, :] * a + cw[2:3, :] * up + cb_ref[...])


def matmul_conv_silu(h, w, conv_w, conv_b):
    n, k = h.shape
    ncol = w.shape[1]
    tm, tn, halo = min(1024, n), 1024, 16
    nh = n // halo
    return pl.pallas_call(
        _mm_conv_silu_kernel,
        out_shape=jax.ShapeDtypeStruct((n, ncol), F32),
        grid=(n // tm, ncol // tn),
        in_specs=[pl.BlockSpec((tm, k), lambda i, j: (i, 0)),
                  pl.BlockSpec((halo, k), lambda i, j: (jnp.maximum(i * (tm // halo) - 1, 0), 0)),
                  pl.BlockSpec((halo, k), lambda i, j: (jnp.minimum((i + 1) * (tm // halo), nh - 1), 0)),
                  pl.BlockSpec((k, tn), lambda i, j: (0, j)),
                  pl.BlockSpec((3, tn), lambda i, j: (0, j)),
                  pl.BlockSpec((1, tn), lambda i, j: (0, j))],
        out_specs=pl.BlockSpec((tm, tn), lambda i, j: (i, j)),
        compiler_params=_cp("parallel", "parallel"),
        name="matmul_conv_silu",
    )(h, h, h, w, conv_w, conv_b.reshape(1, ncol))


def _ssd_kernel(*refs, reverse, final):
    if final:
        (x_ref, b_ref, c_ref, dt_ref, dtb_ref, alog_ref, s0_ref, z_ref, yfw_ref, dsk_ref, nw_ref,
         y_ref, sfin_ref, st_ref) = refs
    else:
        x_ref, b_ref, c_ref, dt_ref, dtb_ref, alog_ref, s0_ref, y_ref, sfin_ref, st_ref = refs
    ci = pl.program_id(0)

    @pl.when(ci == 0)
    def _():
        st_ref[...] = s0_ref[...]

    n = x_ref.shape[0]
    p, ns = SSM_HEAD_DIM, SSM_STATE
    heads = x_ref.shape[1] // p
    hpg = heads // SSM_GROUPS
    gw = hpg * p
    dt = _softplus(dt_ref[...] + dtb_ref[...])
    a = _cumsum_rows(dt * -jnp.exp(alog_ref[...]), reverse)
    a_t, dt_t = a.T, dt.T
    last = 0 if reverse else n - 1
    a_last = a[last:last + 1, :]
    ti = lax.broadcasted_iota(jnp.int32, (n, n), 0)
    si = lax.broadcasted_iota(jnp.int32, (n, n), 1)
    mask = (si >= ti) if reverse else (si <= ti)
    lane = lax.broadcasted_iota(jnp.int32, (n, LANES), 1)
    lo = lane < p
    lo1 = lax.broadcasted_iota(jnp.int32, (1, LANES), 1) < p
    hoff = heads if reverse else 0
    for g in range(SSM_GROUPS):
        bg = b_ref[:, g * ns:(g + 1) * ns].astype(BF16)
        cg = c_ref[:, g * ns:(g + 1) * ns].astype(BF16)
        cb = _dot_nt(cg, bg)
        cs = _dot(cg, st_ref[:, g * gw:(g + 1) * gw].astype(BF16))
        xw, dec, yg = [], [], []
        for j in range(hpg // 2):
            h0 = hoff + g * hpg + 2 * j
            sl = slice(g * gw + j * LANES, g * gw + (j + 1) * LANES)
            x2 = x_ref[:, sl]
            x2b = x2.astype(BF16)
            ys, acol = [], []
            for h in (h0, h0 + 1):
                acol.append(jnp.broadcast_to(a[:, h:h + 1], (n, LANES)))
                seg = jnp.where(mask, acol[-1] - a_t[h:h + 1, :], NEG)
                w = cb * jnp.exp(seg) * dt_t[h:h + 1, :]
                ys.append(_dot(w.astype(BF16), x2b))
            a_rep = jnp.where(lo, acol[0], acol[1])
            dt_rep = jnp.where(lo, dt[:, h0:h0 + 1], dt[:, h0 + 1:h0 + 2])
            al_rep = jnp.where(lo1, a_last[:, h0:h0 + 1], a_last[:, h0 + 1:h0 + 2])
            y2 = jnp.where(lo, ys[0], ys[1]) + cs[:, j * LANES:(j + 1) * LANES] * jnp.exp(a_rep)
            xw.append((x2 * (jnp.exp(al_rep - a_rep) * dt_rep)).astype(BF16))
            dec.append(jnp.exp(al_rep))
            if final:
                y2 = y2 + yfw_ref[:, sl] + dsk_ref[:, sl] * x2
                yg.append(y2 * _silu(z_ref[:, sl]))
            else:
                y_ref[:, sl] = y2
        gs = slice(g * gw, (g + 1) * gw)
        st_ref[:, gs] = st_ref[:, gs] * jnp.concatenate(dec, 1) + _dot_tn(bg, jnp.concatenate(xw, 1))
        if final:
            yg = jnp.concatenate(yg, 1)
            yg = yg * lax.rsqrt(jnp.mean(yg * yg, -1, keepdims=True) + RMS_EPS) * nw_ref[:, gs]
            y_ref[:, gs] = yg.astype(y_ref.dtype)

    @pl.when(ci == pl.num_programs(0) - 1)
    def _():
        sfin_ref[...] = st_ref[...]


def ssd_scan(xbc, z, dt, dt_bias, a_log, s0, reverse, y_fw=None, d_skip=None, norm_w=None):
    n = xbc.shape[0]
    gn = SSM_GROUPS * SSM_STATE
    di = xbc.shape[1] - 2 * gn
    lc = SSM_CHUNK
    nc = n // lc
    final = y_fw is not None
    cidx = (lambda c: nc - 1 - c) if reverse else (lambda c: c)
    wide = lambda col: pl.BlockSpec((lc, di), lambda c: (cidx(c), col))
    vec = lambda w: pl.BlockSpec((1, w), lambda c: (0, 0))
    st = pl.BlockSpec((SSM_STATE, di), lambda c: (0, 0))
    in_specs = [wide(0),
                pl.BlockSpec((lc, gn), lambda c: (cidx(c), di // gn)),
                pl.BlockSpec((lc, gn), lambda c: (cidx(c), di // gn + 1)),
                pl.BlockSpec((lc, LANES), lambda c: (cidx(c), 0)),
                vec(LANES), vec(LANES), st]
    args = [xbc, xbc, xbc, dt, dt_bias, a_log, s0]
    if final:
        in_specs += [wide(0), wide(0), vec(di), vec(di)]
        args += [z, y_fw, d_skip, norm_w]
    return pl.pallas_call(
        functools.partial(_ssd_kernel, reverse=reverse, final=final),
        out_shape=(jax.ShapeDtypeStruct((n, di), BF16 if final else F32),
                   jax.ShapeDtypeStruct((SSM_STATE, di), F32)),
        grid=(nc,),
        in_specs=in_specs,
        out_specs=(wide(0), st),
        scratch_shapes=[pltpu.VMEM((SSM_STATE, di), F32)],
        compiler_params=_cp("arbitrary"),
        name="ssd_bw" if reverse else "ssd_fw",
    )(*args)


def _pad_cols(w, n):
    return jnp.pad(w, ((0, 0), (0, n - w.shape[1])))


def swa_mixer(h_ctx, h_lat, w_in, sink, cos2, sin2, with_ctx):
    w = w_in.astype(BF16)
    qkv_l = matmul(h_lat, w, F32, tn=1280)
    qkv_c = matmul(h_ctx, w, F32, tn=1280)
    o_lat, o_ctx = swa_attention(qkv_l, qkv_c, sink.astype(F32), cos2, sin2, with_ctx)
    return o_ctx, o_lat


def hgrn2_mixer(h_ctx, h_lat, w_in, lower_bound, norm_w, with_ctx):
    w = w_in.astype(BF16)
    hk = HG_HEADS * HG_KDIM
    lb = lower_bound.reshape(1, hk)
    nw = norm_w.reshape(1, hk)
    proj_c = matmul(h_ctx, w, F32, tn=1024)
    proj_l = matmul(h_lat, w, F32, tn=1024)
    zeros = jnp.zeros((HG_HEADS, HG_KDIM, HG_KDIM), F32)
    ofw_c, s_fw = hgrn2_scan(proj_c, lb, zeros, False)
    y_c, s_bw = hgrn2_scan(proj_c, lb, zeros, True, ofw_c, nw)
    ofw_l, _ = hgrn2_scan(proj_l, lb, s_fw, False)
    y_l, _ = hgrn2_scan(proj_l, lb, s_bw, True, ofw_l, nw)
    return (y_c if with_ctx else None), y_l


def mla_mixer(h_ctx, h_lat, w_in, q_norm, kv_norm, w_uq, w_ukv, cos2, sin2, with_ctx):
    t, nctx = h_lat.shape[0], h_ctx.shape[0]
    hh = MLA_HEADS
    ncol = MLA_Q_RANK + MLA_KV_RANK + LANES
    w1 = _pad_cols(w_in, ncol).astype(BF16)
    wq = w_uq.reshape(MLA_Q_RANK, hh, MLA_NOPE + MLA_ROPE)
    wq = jnp.pad(wq, ((0, 0), (0, 0), (0, LANES - MLA_ROPE))).reshape(MLA_Q_RANK, hh * (MLA_NOPE + LANES)).astype(BF16)
    wkv = w_ukv.reshape(MLA_KV_RANK, hh, MLA_NOPE + MLA_V)
    wkv = jnp.concatenate([wkv[:, :, :MLA_NOPE].reshape(MLA_KV_RANK, hh * MLA_NOPE),
                           wkv[:, :, MLA_NOPE:].reshape(MLA_KV_RANK, hh * MLA_V)], 1).astype(BF16)
    zpad = jnp.zeros((t, LANES - 2 * 32), F32)
    cos = jnp.concatenate([jnp.concatenate([cos2, zpad], 1), jnp.ones((nctx, LANES), F32)], 0)
    sin = jnp.concatenate([jnp.concatenate([sin2, zpad], 1), jnp.zeros((nctx, LANES), F32)], 0)
    h = jnp.concatenate([h_lat, h_ctx], 0)
    tm = 1280
    cq, ckv, kpe = mla_norm(matmul(h, w1, F32, tn=ncol, tm=tm), q_norm.reshape(1, -1), kv_norm.reshape(1, -1), cos, sin)
    q = matmul(cq, wq, F32, tn=1024, tm=tm)
    kv = matmul(ckv, wkv, BF16, tn=1024, tm=tm)
    o_ctx = mla_attention(q, cos, sin, kv, kpe, (t, nctx), (t, nctx), tq=nctx, tk=nctx) if with_ctx else None
    o_lat = mla_attention(q, cos, sin, kv, kpe, (0, t), (0, t + nctx), tq=1024, tk=1280)
    return o_ctx, o_lat


def mamba2_mixer(h_ctx, h_lat, w_in, conv_w, conv_b, dt_bias, a_log, d_skip, norm_w, with_ctx):
    di = norm_w.shape[0]
    gn = SSM_GROUPS * SSM_STATE
    heads = di // SSM_HEAD_DIM
    ncv = di + 2 * gn
    w_z, w_xbc, w_dt = (w_in[:, :di].astype(BF16), w_in[:, di:di + ncv].astype(BF16), w_in[:, di + ncv:].astype(BF16))
    dtb = dt_bias.reshape(1, 2 * heads).astype(F32)
    alog = a_log.reshape(1, 2 * heads).astype(F32)
    dsk = jnp.repeat(d_skip.astype(F32), SSM_HEAD_DIM).reshape(1, di)
    nw = norm_w.reshape(1, di)

    def prep(h):
        return (matmul_conv_silu(h, w_xbc, conv_w, conv_b), matmul(h, w_z, F32, tn=1024),
                matmul(h, w_dt, F32, tn=2 * heads))

    xbc_c, z_c, dt_c = prep(h_ctx)
    xbc_l, z_l, dt_l = prep(h_lat)
    zeros = jnp.zeros((SSM_STATE, di), F32)
    yfw_c, s_fw = ssd_scan(xbc_c, z_c, dt_c, dtb, alog, zeros, False)
    y_c, s_bw = ssd_scan(xbc_c, z_c, dt_c, dtb, alog, zeros, True, yfw_c, dsk, nw)
    yfw_l, _ = ssd_scan(xbc_l, z_l, dt_l, dtb, alog, s_fw, False)
    y_l, _ = ssd_scan(xbc_l, z_l, dt_l, dtb, alog, s_bw, True, yfw_l, dsk, nw)
    return (y_c if with_ctx else None), y_l


def kernel(x, c, ctx, c_ctx, ada_w, ada_b, ln_g, ln_b, ffn_up, ffn_conv_w, ffn_conv_b, ffn_down, swa_in, swa_sink, swa_out, hg_in, hg_lb_logits, hg_norm_w, hg_out, mla_in, mla_q_norm, mla_kv_norm, mla_uq, mla_ukv, mla_out, ssm_in, ssm_conv_w, ssm_conv_b, ssm_dt_bias, ssm_A_log, ssm_D, ssm_norm_w, ssm_out):
    batch, t, d = x.shape
    assert batch == 1
    depth = ada_w.shape[0]
    x_lat, x_ctx = x[0], ctx[0]

    probs = jax.nn.softmax(hg_lb_logits.astype(F32), axis=0)
    lower_bounds = jnp.cumsum(probs, axis=0) - probs[0]

    cvec = jnp.zeros((8, d), F32).at[0].set(c[0]).at[1].set(c_ctx)
    mods = ada_modulation(cvec, ada_w, ada_b)

    def mod(i, stream, j):
        return mods[i, stream:stream + 1, j * d:(j + 1) * d]

    cos2, sin2 = rope_tables(t)
    zero = jnp.zeros((1, d), F32)
    h_lat = modulate(x_lat, mod(0, 0, 0), mod(0, 0, 1))
    h_ctx = modulate(x_ctx, mod(0, 1, 0), mod(0, 1, 1))
    for i in range(depth):
        kind, j = i % N_MIXERS, i // N_MIXERS
        with_ctx = i < depth - 1
        if kind == 0:
            y_ctx, y_lat = swa_mixer(h_ctx, h_lat, swa_in[j], swa_sink[j], cos2, sin2, with_ctx)
            w_out = swa_out[j]
        elif kind == 1:
            y_ctx, y_lat = hgrn2_mixer(h_ctx, h_lat, hg_in[j], lower_bounds[i], hg_norm_w[j], with_ctx)
            w_out = hg_out[j]
        elif kind == 2:
            y_ctx, y_lat = mla_mixer(h_ctx, h_lat, mla_in[j], mla_q_norm[j], mla_kv_norm[j], mla_uq[j], mla_ukv[j],
                                     cos2, sin2, with_ctx)
            w_out = mla_out[j]
        else:
            y_ctx, y_lat = mamba2_mixer(h_ctx, h_lat, ssm_in[j], ssm_conv_w[j], ssm_conv_b[j], ssm_dt_bias[j],
                                        ssm_A_log[j], ssm_D[j], ssm_norm_w[j], with_ctx)
            w_out = ssm_out[j]
        w_out = w_out.astype(BF16)
        w_up, w_down = ffn_up[i].astype(BF16), ffn_down[i].astype(BF16)
        g0, b0, g1, b1 = ln_g[i, 0:1], ln_b[i, 0:1], ln_g[i, 1:2], ln_b[i, 1:2]
        streams = [(0, x_lat, y_lat)] + ([(1, x_ctx, y_ctx)] if with_ctx else [])
        outs = []
        for s, xs, ys in streams:
            xs, hf = matmul_ln(ys, w_out, xs, mod(i, s, 2), g0, b0, mod(i, s, 3), mod(i, s, 4))
            nsh, nsc = (mod(i + 1, s, 0), mod(i + 1, s, 1)) if i + 1 < depth else (zero, zero)
            outs.append(conv_ffn_ln(hf, w_up, ffn_conv_w[i], ffn_conv_b[i], w_down, xs, mod(i, s, 5), g1, b1, nsh, nsc))
        x_lat, h_lat = outs[0]
        if with_ctx:
            x_ctx, h_ctx = outs[1]
    return x_lat[None]
```

```python
import functools

import jax
import jax.numpy as jnp
from jax import lax
from jax.experimental import pallas as pl
from jax.experimental.pallas import tpu as pltpu

F32, BF16 = jnp.float32, jnp.bfloat16

DEPTH, N_MIXERS = 4, 4
GRID_W = 64
ALPHA = (2.0 * DEPTH) ** 0.25
LN_EPS, RMS_EPS, ROPE_BASE = 1e-5, 1e-6, 10000.0
SWA_HEADS, SWA_KV_HEADS, SWA_HEAD_DIM, SWA_BLOCK = 32, 4, 64, 128
HG_HEADS, HG_KDIM = 16, 128
MLA_HEADS, MLA_Q_RANK, MLA_KV_RANK, MLA_NOPE, MLA_ROPE, MLA_V = 16, 512, 512, 128, 64, 128
SSM_HEAD_DIM, SSM_STATE, SSM_GROUPS, SSM_CHUNK = 64, 128, 8, 128
HG_CHUNK = 64
LANES = 128
NEG = -1e30
LOG2E = 1.4426950408889634
VMEM_LIMIT = 56 * 1024 * 1024


def _cp(*sem):
    return pltpu.CompilerParams(dimension_semantics=sem, vmem_limit_bytes=VMEM_LIMIT)


def _dot(a, b):
    return jnp.dot(a, b, preferred_element_type=F32)


def _dot_nt(a, b):
    return lax.dot_general(a, b, (((1,), (1,)), ((), ())), preferred_element_type=F32)


def _dot_tn(a, b):
    return lax.dot_general(a, b, (((0,), (0,)), ((), ())), preferred_element_type=F32)


def _sigmoid(x):
    return 1.0 / (1.0 + jnp.exp(-x))


def _silu(x):
    return x * _sigmoid(x)


def _softplus(x):
    return jnp.maximum(x, 0.0) + jnp.log(1.0 + jnp.exp(-jnp.abs(x)))


def _ln_mod(z, g, b):
    mu = jnp.mean(z, -1, keepdims=True)
    d = z - mu
    var = jnp.mean(d * d, -1, keepdims=True)
    return d * lax.rsqrt(var + LN_EPS) * g + b


def _shift_rows(x, prev_row, next_row):
    n = x.shape[0]
    row = lax.broadcasted_iota(jnp.int32, (n, 1), 0)
    dn = jnp.where(row == 0, prev_row, pltpu.roll(x, 1, 0))
    up = jnp.where(row == n - 1, next_row, pltpu.roll(x, n - 1, 0))
    return dn, up


def _cumsum_rows(x, reverse):
    n = x.shape[0]
    row = lax.broadcasted_iota(jnp.int32, (n, 1), 0)
    s = 1
    while s < n:
        if reverse:
            x = x + jnp.where(row < n - s, pltpu.roll(x, n - s, 0), 0.0)
        else:
            x = x + jnp.where(row >= s, pltpu.roll(x, s, 0), 0.0)
        s *= 2
    return x


def _ada_kernel(c_ref, w_ref, b_ref, o_ref):
    a = _silu(c_ref[...]).astype(BF16)
    o_ref[0] = _dot(a, w_ref[0].astype(BF16)) + b_ref[0]


def ada_modulation(cvec, ada_w, ada_b):
    depth, d, n = ada_w.shape
    tn = 1024
    return pl.pallas_call(
        _ada_kernel,
        out_shape=jax.ShapeDtypeStruct((depth, 8, n), F32),
        grid=(depth, n // tn),
        in_specs=[pl.BlockSpec((8, d), lambda i, j: (0, 0)),
                  pl.BlockSpec((1, d, tn), lambda i, j: (i, 0, j)),
                  pl.BlockSpec((1, 1, tn), lambda i, j: (i, 0, j))],
        out_specs=pl.BlockSpec((1, 8, tn), lambda i, j: (i, 0, j)),
        compiler_params=_cp("parallel", "parallel"),
        name="ada_modulation",
    )(cvec, ada_w, ada_b.reshape(depth, 1, n))


def _modulate_kernel(x_ref, sh_ref, sc_ref, o_ref):
    o_ref[...] = (x_ref[...] * (1.0 + sc_ref[...]) + sh_ref[...]).astype(o_ref.dtype)


def modulate(x, shift, scale):
    m, d = x.shape
    tm = min(512, m)
    vec = pl.BlockSpec((1, d), lambda i: (0, 0))
    return pl.pallas_call(
        _modulate_kernel,
        out_shape=jax.ShapeDtypeStruct((m, d), BF16),
        grid=(m // tm,),
        in_specs=[pl.BlockSpec((tm, d), lambda i: (i, 0)), vec, vec],
        out_specs=pl.BlockSpec((tm, d), lambda i: (i, 0)),
        compiler_params=_cp("parallel"),
        name="modulate",
    )(x, shift, scale)


def _mm_kernel(a_ref, w_ref, o_ref):
    o_ref[...] = _dot(a_ref[...], w_ref[...]).astype(o_ref.dtype)


def matmul(a, w, out_dtype, tn, tm=1024):
    m, k = a.shape
    n = w.shape[1]
    tm = min(tm, m)
    assert m % tm == 0 and n % tn == 0
    return pl.pallas_call(
        _mm_kernel,
        out_shape=jax.ShapeDtypeStruct((m, n), out_dtype),
        grid=(m // tm, n // tn),
        in_specs=[pl.BlockSpec((tm, k), lambda i, j: (i, 0)),
                  pl.BlockSpec((k, tn), lambda i, j: (0, j))],
        out_specs=pl.BlockSpec((tm, tn), lambda i, j: (i, j)),
        compiler_params=_cp("parallel", "parallel"),
        name="matmul",
    )(a, w)


def _mm_ln_kernel(a_ref, w_ref, x_ref, gate_ref, g_ref, b_ref, sh_ref, sc_ref, xo_ref, ho_ref, acc_ref):
    k = pl.program_id(1)

    @pl.when(k == 0)
    def _():
        acc_ref[...] = jnp.zeros_like(acc_ref)

    acc_ref[...] += _dot(a_ref[...], w_ref[...])

    @pl.when(k == pl.num_programs(1) - 1)
    def _():
        xn = _ln_mod(ALPHA * x_ref[...] + gate_ref[...] * acc_ref[...], g_ref[...], b_ref[...])
        xo_ref[...] = xn
        ho_ref[...] = (xn * (1.0 + sc_ref[...]) + sh_ref[...]).astype(ho_ref.dtype)


def matmul_ln(a, w, x, gate, g, b, shift, scale):
    m, k = a.shape
    d = w.shape[1]
    tm, tk = min(512, m), 2048
    vec = pl.BlockSpec((1, d), lambda i, kk: (0, 0))
    row = pl.BlockSpec((tm, d), lambda i, kk: (i, 0))
    return pl.pallas_call(
        _mm_ln_kernel,
        out_shape=(jax.ShapeDtypeStruct((m, d), F32), jax.ShapeDtypeStruct((m, d), BF16)),
        grid=(m // tm, k // tk),
        in_specs=[pl.BlockSpec((tm, tk), lambda i, kk: (i, kk)),
                  pl.BlockSpec((tk, d), lambda i, kk: (kk, 0)),
                  row, vec, vec, vec, vec, vec],
        out_specs=(row, row),
        scratch_shapes=[pltpu.VMEM((tm, d), F32)],
        compiler_params=_cp("parallel", "arbitrary"),
        name="matmul_ln",
    )(a, w, x, gate, g, b, shift, scale)


def _ffn_kernel(h_ref, hp_ref, hn_ref, wa_ref, wv_ref, cw_ref, cb_ref, wd_ref, x_ref, gate_ref, g_ref, b_ref,
                sh_ref, sc_ref, xo_ref, ho_ref):
    i, f = pl.program_id(0), pl.program_id(1)
    acc_ref = xo_ref

    @pl.when(f == 0)
    def _():
        acc_ref[...] = jnp.zeros_like(acc_ref)

    wa = wa_ref[...]
    nrow = hp_ref.shape[0]
    a_prev = _dot(hp_ref[...], wa)[nrow - 1:nrow, :]
    a_next = _dot(hn_ref[...], wa)[0:1, :]
    a_prev = jnp.where(i > 0, a_prev, 0.0)
    a_next = jnp.where(i < pl.num_programs(0) - 1, a_next, 0.0)
    cw = cw_ref[...]
    a = _dot(h_ref[...], wa)
    v = _dot(h_ref[...], wv_ref[...])
    dn, up = _shift_rows(a, a_prev, a_next)
    cv = cw[0:1, :] * dn + cw[1:2, :] * a + cw[2:3, :] * up + cb_ref[...]
    u = (_silu(cv) * v).astype(BF16)
    acc_ref[...] += _dot(u, wd_ref[...])

    @pl.when(f == pl.num_programs(1) - 1)
    def _():
        xn = _ln_mod(ALPHA * x_ref[...] + gate_ref[...] * acc_ref[...], g_ref[...], b_ref[...])
        xo_ref[...] = xn
        ho_ref[...] = (xn * (1.0 + sc_ref[...]) + sh_ref[...]).astype(ho_ref.dtype)


def conv_ffn_ln(h, w_up, conv_w, conv_b, w_down, layer, x, gate, g, b, shift, scale):
    m, d = h.shape
    f = w_down.shape[1]
    tm, tf, halo = min(1024, m), 512, 16
    nf = f // tf
    nh = m // halo
    vec = pl.BlockSpec((1, d), lambda i, j: (0, 0))
    row = pl.BlockSpec((tm, d), lambda i, j: (i, 0))
    once = pl.BlockSpec((tm, d), lambda i, j: (i, 0), pipeline_mode=pl.Buffered(1))
    return pl.pallas_call(
        _ffn_kernel,
        out_shape=(jax.ShapeDtypeStruct((m, d), F32), jax.ShapeDtypeStruct((m, d), BF16)),
        grid=(m // tm, nf),
        in_specs=[row,
                  pl.BlockSpec((halo, d), lambda i, j: (jnp.maximum(i * (tm // halo) - 1, 0), 0)),
                  pl.BlockSpec((halo, d), lambda i, j: (jnp.minimum((i + 1) * (tm // halo), nh - 1), 0)),
                  pl.BlockSpec((None, d, tf), lambda i, j: (layer, 0, j)),
                  pl.BlockSpec((None, d, tf), lambda i, j: (layer, 0, nf + j)),
                  pl.BlockSpec((3, tf), lambda i, j: (0, j)),
                  pl.BlockSpec((1, tf), lambda i, j: (0, j)),
                  pl.BlockSpec((None, tf, d), lambda i, j: (layer, j, 0)),
                  once, vec, vec, vec, vec, vec],
        out_specs=(once, once),
        compiler_params=_cp("parallel", "arbitrary"),
        name="conv_ffn_ln",
    )(h, h, h, w_up, w_up, conv_w, conv_b.reshape(1, f), w_down, x, gate, g, b, shift, scale)


def rope_tables(n_tokens):
    rows = n_tokens // GRID_W
    row = jnp.repeat(jnp.arange(rows, dtype=F32), GRID_W)
    col = jnp.tile(jnp.arange(GRID_W, dtype=F32), rows)
    n_axis = 16
    inv_freq = ROPE_BASE ** (-jnp.arange(n_axis, dtype=F32) / n_axis)
    ang = jnp.concatenate([row[:, None] * inv_freq, col[:, None] * inv_freq], -1)
    cos, sin = jnp.cos(ang), jnp.sin(ang)
    return jnp.concatenate([cos, cos], -1), jnp.concatenate([-sin, sin], -1)


def _rope2(x, cos, sin):
    lane = lax.broadcasted_iota(jnp.int32, x.shape, 1)
    rot = jnp.where((lane & 63) < 32, pltpu.roll(x, 96, 1), pltpu.roll(x, 32, 1))
    return x * cos + rot * sin


def _split_halves(x, upper):
    lane = lax.broadcasted_iota(jnp.int32, x.shape, 1)
    if upper:
        hi = jnp.where(lane >= 64, x, 0.0)
        return pltpu.roll(hi, 64, 1), hi
    lo = jnp.where(lane < 64, x, 0.0)
    return lo, pltpu.roll(lo, 64, 1)


def _block_diag(pieces, upper):
    halves = [_split_halves(x, upper) for x in pieces]
    return jnp.concatenate([h[0] for h in halves] + [h[1] for h in halves], 0).astype(BF16)


def _swa_groups(sink_ref, q_ref, keys, vals, bias, o_ref, rope):
    blk = q_ref.shape[0]
    hpg = SWA_HEADS // SWA_KV_HEADS // 2
    rowp = lax.shift_right_logical(lax.broadcasted_iota(jnp.int32, (hpg * blk, 1), 0), blk.bit_length() - 1)
    lane = lax.broadcasted_iota(jnp.int32, (hpg * blk, LANES), 1)
    for g in range(SWA_KV_HEADS):
        qs = []
        for pp in range(hpg):
            q2 = q_ref[:, (g * hpg + pp) * LANES:(g * hpg + pp + 1) * LANES]
            if rope is not None:
                q2 = _rope2(q2, *rope)
            qs.append(q2 * (SWA_HEAD_DIM ** -0.5 * LOG2E))
        s = _dot_nt(jnp.concatenate(qs, 0).astype(BF16), keys[g])
        nk = s.shape[1] // 2
        ps, dens = [], []
        for half in range(2):
            sh = s[:, half * nk:(half + 1) * nk]
            if bias is not None:
                sh = sh + bias
            sink = jnp.zeros((hpg * blk, 1), F32)
            for pp in range(hpg):
                sink = jnp.where(rowp == pp, sink_ref[2 * (g * hpg + pp) + half] * LOG2E, sink)
            m = jnp.maximum(jnp.max(sh, -1, keepdims=True), sink)
            p = jnp.exp2(sh - m)
            dens.append(jnp.sum(p, -1, keepdims=True) + jnp.exp2(sink - m))
            ps.append(p.astype(BF16))
        o = _dot(jnp.concatenate(ps, 1), vals[g]) / jnp.where(lane < 64, dens[0], dens[1])
        for pp in range(hpg):
            o_ref[:, (g * hpg + pp) * LANES:(g * hpg + pp + 1) * LANES] = o[pp * blk:(pp + 1) * blk].astype(o_ref.dtype)


def _swa_lat_kernel(sink_ref, q_ref, kp_ref, kc_ref, kn_ref, vp_ref, vc_ref, vn_ref, kx_ref, vx_ref,
                    cq_ref, sq_ref, cp_ref, sp_ref, cn_ref, sn_ref, o_ref):
    i, nb = pl.program_id(0), pl.num_programs(0)
    blk = q_ref.shape[0]
    nctx = kx_ref.shape[0]
    keys, vals = [], []
    for g in range(SWA_KV_HEADS):
        sl = slice((g // 2) * LANES, (g // 2 + 1) * LANES)
        up = g % 2 == 1
        keys.append(_block_diag([_rope2(kp_ref[:, sl], cp_ref[...], sp_ref[...]),
                                 _rope2(kc_ref[:, sl], cq_ref[...], sq_ref[...]),
                                 _rope2(kn_ref[:, sl], cn_ref[...], sn_ref[...]), kx_ref[:, sl]], up))
        vals.append(_block_diag([vp_ref[:, sl], vc_ref[:, sl], vn_ref[:, sl], vx_ref[:, sl]], up))
    nk = 3 * blk + nctx
    rows = (SWA_HEADS // SWA_KV_HEADS // 2) * blk
    r = lax.broadcasted_iota(jnp.int32, (rows, nk), 0) & (blk - 1)
    c = lax.broadcasted_iota(jnp.int32, (rows, nk), 1)
    has_prev = jnp.where(i > 0, 0.0, NEG)
    has_next = jnp.where(i < nb - 1, 0.0, NEG)
    bias = jnp.where(c < blk, jnp.where(c >= r, has_prev, NEG),
                     jnp.where(c < 2 * blk, 0.0,
                               jnp.where(c < 3 * blk, jnp.where(c - 2 * blk <= r, has_next, NEG), 0.0)))
    _swa_groups(sink_ref, q_ref, keys, vals, bias, o_ref, (cq_ref[...], sq_ref[...]))


def _swa_ctx_kernel(sink_ref, q_ref, kx_ref, vx_ref, o_ref):
    keys, vals = [], []
    for g in range(SWA_KV_HEADS):
        sl = slice((g // 2) * LANES, (g // 2 + 1) * LANES)
        keys.append(_block_diag([kx_ref[:, sl]], g % 2 == 1))
        vals.append(_block_diag([vx_ref[:, sl]], g % 2 == 1))
    _swa_groups(sink_ref, q_ref, keys, vals, None, o_ref, None)


def swa_attention(qkv_lat, qkv_ctx, sink, cos2, sin2, with_ctx):
    t, nctx = qkv_lat.shape[0], qkv_ctx.shape[0]
    blk, dq = SWA_BLOCK, SWA_HEADS * SWA_HEAD_DIM
    dkv = SWA_KV_HEADS * SWA_HEAD_DIM
    kcol, vcol = dq // dkv, dq // dkv + 1
    nb = t // blk
    cos = jnp.concatenate([cos2, cos2], -1)
    sin = jnp.concatenate([sin2, sin2], -1)
    smem = pl.BlockSpec(memory_space=pltpu.SMEM)
    prev = lambda i: jnp.maximum(i - 1, 0)
    nxt = lambda i: jnp.minimum(i + 1, nb - 1)
    tab = lambda f: pl.BlockSpec((blk, LANES), lambda i: (f(i), 0))
    kv = lambda f, col: pl.BlockSpec((blk, dkv), lambda i: (f(i), col))
    same = lambda i: i
    o_lat = pl.pallas_call(
        _swa_lat_kernel,
        out_shape=jax.ShapeDtypeStruct((t, dq), BF16),
        grid=(nb,),
        in_specs=[smem, pl.BlockSpec((blk, dq), lambda i: (i, 0)),
                  kv(prev, kcol), kv(same, kcol), kv(nxt, kcol), kv(prev, vcol), kv(same, vcol), kv(nxt, vcol),
                  pl.BlockSpec((nctx, dkv), lambda i: (0, kcol)), pl.BlockSpec((nctx, dkv), lambda i: (0, vcol)),
                  tab(same), tab(same), tab(prev), tab(prev), tab(nxt), tab(nxt)],
        out_specs=pl.BlockSpec((blk, dq), lambda i: (i, 0)),
        compiler_params=_cp("parallel"),
        name="swa_lat",
    )(sink, qkv_lat, qkv_lat, qkv_lat, qkv_lat, qkv_lat, qkv_lat, qkv_lat, qkv_ctx, qkv_ctx,
      cos, sin, cos, sin, cos, sin)
    if not with_ctx:
        return o_lat, None
    o_ctx = pl.pallas_call(
        _swa_ctx_kernel,
        out_shape=jax.ShapeDtypeStruct((nctx, dq), BF16),
        grid=(nctx // blk,),
        in_specs=[smem, pl.BlockSpec((blk, dq), lambda i: (i, 0)),
                  pl.BlockSpec((nctx, dkv), lambda i: (0, kcol)), pl.BlockSpec((nctx, dkv), lambda i: (0, vcol))],
        out_specs=pl.BlockSpec((blk, dq), lambda i: (i, 0)),
        compiler_params=_cp("parallel"),
        name="swa_ctx",
    )(sink, qkv_ctx, qkv_ctx, qkv_ctx)
    return o_lat, o_ctx


def _hg_kernel(*refs, reverse, final):
    if final:
        q_ref, f_ref, v_ref, lb_ref, s0_ref, g_ref, ofw_ref, nw_ref, o_ref, sfin_ref, st_ref = refs
    else:
        q_ref, f_ref, v_ref, lb_ref, s0_ref, o_ref, sfin_ref, st_ref = refs
    c = pl.program_id(0)

    @pl.when(c == 0)
    def _():
        st_ref[...] = s0_ref[...]

    n = q_ref.shape[0]
    kd = HG_KDIM
    lb = lb_ref[...]
    q = _silu(q_ref[...]) * kd ** -0.5
    sg = _sigmoid(f_ref[...])
    lf = jnp.log(lb + (1.0 - lb) * sg)
    k = (1.0 - lb) * (1.0 - sg)
    v = v_ref[...].astype(BF16)
    b = _cumsum_rows(lf, reverse)
    mid = n // 2 if reverse else n // 2 - 1
    last = 0 if reverse else n - 1
    r = b[mid:mid + 1, :]
    b_last = b[last:last + 1, :]
    qe = (q * jnp.exp(b - r)).astype(BF16)
    ke = (k * jnp.exp(r - b)).astype(BF16)
    qs = (q * jnp.exp(b)).astype(BF16)
    kl = (k * jnp.exp(b_last - b)).astype(BF16)
    dec = jnp.exp(b_last)
    ti = lax.broadcasted_iota(jnp.int32, (n, n), 0)
    si = lax.broadcasted_iota(jnp.int32, (n, n), 1)
    mask = (si >= ti) if reverse else (si <= ti)
    for h in range(q_ref.shape[1] // kd):
        sl = slice(h * kd, (h + 1) * kd)
        att = jnp.where(mask, _dot_nt(qe[:, sl], ke[:, sl]), 0.0).astype(BF16)
        st = st_ref[h]
        o = _dot(att, v[:, sl]) + _dot_nt(qs[:, sl], st.astype(BF16))
        st_ref[h] = st * dec[:, sl] + _dot_tn(v[:, sl], kl[:, sl])
        if final:
            o = o + ofw_ref[:, sl]
            o = o * lax.rsqrt(jnp.mean(o * o, -1, keepdims=True) + RMS_EPS) * nw_ref[:, sl]
            o_ref[:, sl] = (o * _silu(g_ref[:, sl])).astype(o_ref.dtype)
        else:
            o_ref[:, sl] = o

    @pl.when(c == pl.num_programs(0) - 1)
    def _():
        sfin_ref[...] = st_ref[...]


def hgrn2_scan(proj, lb, s0, reverse, o_fw=None, norm_w=None):
    n = proj.shape[0]
    hk = HG_HEADS * HG_KDIM
    lc = HG_CHUNK
    nc = n // lc
    final = o_fw is not None
    cidx = (lambda c: nc - 1 - c) if reverse else (lambda c: c)
    blk = lambda col: pl.BlockSpec((lc, hk), lambda c: (cidx(c), col))
    vec = pl.BlockSpec((1, hk), lambda c: (0, 0))
    st = pl.BlockSpec((HG_HEADS, HG_KDIM, HG_KDIM), lambda c: (0, 0, 0))
    in_specs = [blk(0), blk(2 if reverse else 1), blk(3), vec, st]
    args = [proj, proj, proj, lb, s0]
    if final:
        in_specs += [blk(4), blk(0), vec]
        args += [proj, o_fw, norm_w]
    return pl.pallas_call(
        functools.partial(_hg_kernel, reverse=reverse, final=final),
        out_shape=(jax.ShapeDtypeStruct((n, hk), BF16 if final else F32),
                   jax.ShapeDtypeStruct((HG_HEADS, HG_KDIM, HG_KDIM), F32)),
        grid=(nc,),
        in_specs=in_specs,
        out_specs=(blk(0), st),
        scratch_shapes=[pltpu.VMEM((HG_HEADS, HG_KDIM, HG_KDIM), F32)],
        compiler_params=_cp("arbitrary"),
        name="hgrn2_bw" if reverse else "hgrn2_fw",
    )(*args)


def _mla_norm_kernel(p_ref, qn_ref, kvn_ref, cos_ref, sin_ref, cq_ref, ckv_ref, kpe_ref):
    cq = p_ref[:, :MLA_Q_RANK]
    ckv = p_ref[:, MLA_Q_RANK:MLA_Q_RANK + MLA_KV_RANK]
    kp = p_ref[:, MLA_Q_RANK + MLA_KV_RANK:]
    cq_ref[...] = (cq * lax.rsqrt(jnp.mean(cq * cq, -1, keepdims=True) + RMS_EPS) * qn_ref[...]).astype(BF16)
    ckv_ref[...] = (ckv * lax.rsqrt(jnp.mean(ckv * ckv, -1, keepdims=True) + RMS_EPS) * kvn_ref[...]).astype(BF16)
    rot = pltpu.roll(kp, 96, 1) + pltpu.roll(kp, 32, 1)
    kpe_ref[...] = (kp * cos_ref[...] + rot * sin_ref[...]).astype(BF16)


def mla_norm(proj, q_norm, kv_norm, cos, sin):
    m = proj.shape[0]
    tm = 640 if m % 640 == 0 else 128
    assert m % tm == 0
    return pl.pallas_call(
        _mla_norm_kernel,
        out_shape=(jax.ShapeDtypeStruct((m, MLA_Q_RANK), BF16), jax.ShapeDtypeStruct((m, MLA_KV_RANK), BF16),
                   jax.ShapeDtypeStruct((m, LANES), BF16)),
        grid=(m // tm,),
        in_specs=[pl.BlockSpec((tm, proj.shape[1]), lambda i: (i, 0)),
                  pl.BlockSpec((1, MLA_Q_RANK), lambda i: (0, 0)), pl.BlockSpec((1, MLA_KV_RANK), lambda i: (0, 0)),
                  pl.BlockSpec((tm, LANES), lambda i: (i, 0)), pl.BlockSpec((tm, LANES), lambda i: (i, 0))],
        out_specs=(pl.BlockSpec((tm, MLA_Q_RANK), lambda i: (i, 0)), pl.BlockSpec((tm, MLA_KV_RANK), lambda i: (i, 0)),
                   pl.BlockSpec((tm, LANES), lambda i: (i, 0))),
        compiler_params=_cp("parallel"),
        name="mla_norm",
    )(proj, q_norm, kv_norm, cos, sin)


def _mla_attn_kernel(q_ref, cos_ref, sin_ref, kn_ref, kpe_ref, v_ref, o_ref,
                     q_scr, s0_scr, s1_scr, p0_scr, p1_scr, a0_scr, a1_scr, m_scr, acc_scr, *, tk):
    n = kn_ref.shape[0] // tk
    s_scr, p_scr, a_scr = (s0_scr, s1_scr), (p0_scr, p1_scr), (a0_scr, a1_scr)

    qp = q_ref[:, MLA_NOPE:]
    rot = pltpu.roll(qp, 96, 1) + pltpu.roll(qp, 32, 1)
    qp = qp * cos_ref[...] + rot * sin_ref[...]
    scale = (MLA_NOPE + MLA_ROPE) ** -0.5 * LOG2E
    q_scr[...] = (jnp.concatenate([q_ref[:, :MLA_NOPE], qp], 1) * scale).astype(BF16)
    m_scr[...] = jnp.full_like(m_scr, NEG)
    acc_scr[...] = jnp.zeros_like(acc_scr)
    ones = jnp.ones((tk, LANES), BF16)

    def rows(j):
        return pl.ds(j * tk, tk) if isinstance(j, int) else pl.ds(pl.multiple_of(j * tk, tk), tk)

    def scores(j, slot):
        k = jnp.concatenate([kn_ref[rows(j), :], kpe_ref[rows(j), :]], 1)
        s_scr[slot][...] = _dot_nt(q_scr[...], k)

    def softmax(slot):
        s = s_scr[slot][...]
        m_old = m_scr[...]
        m_new = jnp.maximum(m_old, jnp.max(s, -1, keepdims=True))
        p_scr[slot][...] = jnp.exp2(s - m_new).astype(BF16)
        a_scr[slot][...] = jnp.exp2(m_old - m_new)
        m_scr[...] = m_new

    def weighted(j, slot):
        v1 = jnp.concatenate([v_ref[rows(j), :], ones], 1)
        acc_scr[...] = a_scr[slot][...] * acc_scr[...] + _dot(p_scr[slot][...], v1)

    def steady(t, a, b):
        scores(t + 1, a)
        softmax(b)
        weighted(t - 1, a)

    scores(0, 0)
    if n > 1:
        scores(1, 1)
        softmax(0)
        pairs = (n - 2) // 2

        def pair(i, carry):
            steady(1 + 2 * i, 0, 1)
            steady(2 + 2 * i, 1, 0)
            return carry

        if pairs > 0:
            lax.fori_loop(0, pairs, pair, 0)
        if (n - 2) % 2:
            steady(n - 2, 0, 1)
        weighted(n - 2, (n - 2) % 2)
    softmax((n - 1) % 2)
    weighted(n - 1, (n - 1) % 2)
    o_ref[...] = (acc_scr[:, :MLA_V] / acc_scr[:, MLA_V:]).astype(o_ref.dtype)


def mla_attention(q, cos, sin, kv, kpe, q_rows, k_rows, tq, tk):
    (q0, nq), (k0, nk) = q_rows, k_rows
    hq = MLA_NOPE + LANES
    assert nk % tk == 0 and nq % tq == 0 and q0 % tq == 0 and k0 % nk == 0
    qb, kb = q0 // tq, k0 // nk
    return pl.pallas_call(
        functools.partial(_mla_attn_kernel, tk=tk),
        out_shape=jax.ShapeDtypeStruct((nq, MLA_HEADS * MLA_V), BF16),
        grid=(MLA_HEADS, nq // tq),
        in_specs=[pl.BlockSpec((tq, hq), lambda h, i: (qb + i, h)),
                  pl.BlockSpec((tq, LANES), lambda h, i: (qb + i, 0)),
                  pl.BlockSpec((tq, LANES), lambda h, i: (qb + i, 0)),
                  pl.BlockSpec((nk, MLA_NOPE), lambda h, i: (kb, h)),
                  pl.BlockSpec((nk, LANES), lambda h, i: (kb, 0)),
                  pl.BlockSpec((nk, MLA_V), lambda h, i: (kb, MLA_HEADS + h))],
        out_specs=pl.BlockSpec((tq, MLA_V), lambda h, i: (i, h)),
        scratch_shapes=[pltpu.VMEM((tq, hq), BF16),
                        pltpu.VMEM((tq, tk), F32), pltpu.VMEM((tq, tk), F32),
                        pltpu.VMEM((tq, tk), BF16), pltpu.VMEM((tq, tk), BF16),
                        pltpu.VMEM((tq, 1), F32), pltpu.VMEM((tq, 1), F32), pltpu.VMEM((tq, 1), F32),
                        pltpu.VMEM((tq, MLA_V + LANES), F32)],
        compiler_params=_cp("parallel", "parallel"),
        name="mla_attention",
    )(q, cos, sin, kv, kpe, kv)


def _mm_conv_silu_kernel(h_ref, hp_ref, hn_ref, w_ref, cw_ref, cb_ref, o_ref):
    i = pl.program_id(0)
    w = w_ref[...]
    nrow = hp_ref.shape[0]
    a = _dot(h_ref[...], w)
    a_prev = jnp.where(i > 0, _dot(hp_ref[...], w)[nrow - 1:nrow, :], 0.0)
    a_next = jnp.where(i < pl.num_programs(0) - 1, _dot(hn_ref[...], w)[0:1, :], 0.0)
    dn, up = _shift_rows(a, a_prev, a_next)
    cw = cw_ref[...]
    o_ref[...] = _silu(cw[0:1, :] * dn + cw[1:2, :] * a + cw[2:3, :] * up + cb_ref[...])


def matmul_conv_silu(h, w, conv_w, conv_b):
    n, k = h.shape
    ncol = w.shape[1]
    tm, tn, halo = min(1024, n), 1024, 16
    nh = n // halo
    return pl.pallas_call(
        _mm_conv_silu_kernel,
        out_shape=jax.ShapeDtypeStruct((n, ncol), F32),
        grid=(n // tm, ncol // tn),
        in_specs=[pl.BlockSpec((tm, k), lambda i, j: (i, 0)),
                  pl.BlockSpec((halo, k), lambda i, j: (jnp.maximum(i * (tm // halo) - 1, 0), 0)),
                  pl.BlockSpec((halo, k), lambda i, j: (jnp.minimum((i + 1) * (tm // halo), nh - 1), 0)),
                  pl.BlockSpec((k, tn), lambda i, j: (0, j)),
                  pl.BlockSpec((3, tn), lambda i, j: (0, j)),
                  pl.BlockSpec((1, tn), lambda i, j: (0, j))],
        out_specs=pl.BlockSpec((tm, tn), lambda i, j: (i, j)),
        compiler_params=_cp("parallel", "parallel"),
        name="matmul_conv_silu",
    )(h, h, h, w, conv_w, conv_b.reshape(1, ncol))


def _ssd_kernel(*refs, reverse, final):
    if final:
        (x_ref, b_ref, c_ref, dt_ref, dtb_ref, alog_ref, s0_ref, z_ref, yfw_ref, dsk_ref, nw_ref,
         y_ref, sfin_ref, st_ref) = refs
    else:
        x_ref, b_ref, c_ref, dt_ref, dtb_ref, alog_ref, s0_ref, y_ref, sfin_ref, st_ref = refs
    ci = pl.program_id(0)

    @pl.when(ci == 0)
    def _():
        st_ref[...] = s0_ref[...]

    n = x_ref.shape[0]
    p, ns = SSM_HEAD_DIM, SSM_STATE
    heads = x_ref.shape[1] // p
    hpg = heads // SSM_GROUPS
    gw = hpg * p
    dt = _softplus(dt_ref[...] + dtb_ref[...])
    a = _cumsum_rows(dt * -jnp.exp(alog_ref[...]), reverse)
    a_t, dt_t = a.T, dt.T
    last = 0 if reverse else n - 1
    a_last = a[last:last + 1, :]
    ti = lax.broadcasted_iota(jnp.int32, (n, n), 0)
    si = lax.broadcasted_iota(jnp.int32, (n, n), 1)
    mask = (si >= ti) if reverse else (si <= ti)
    lane = lax.broadcasted_iota(jnp.int32, (n, LANES), 1)
    lo = lane < p
    lo1 = lax.broadcasted_iota(jnp.int32, (1, LANES), 1) < p
    hoff = heads if reverse else 0
    for g in range(SSM_GROUPS):
        bg = b_ref[:, g * ns:(g + 1) * ns].astype(BF16)
        cg = c_ref[:, g * ns:(g + 1) * ns].astype(BF16)
        cb = _dot_nt(cg, bg)
        cs = _dot(cg, st_ref[:, g * gw:(g + 1) * gw].astype(BF16))
        xw, dec, yg = [], [], []
        for j in range(hpg // 2):
            h0 = hoff + g * hpg + 2 * j
            sl = slice(g * gw + j * LANES, g * gw + (j + 1) * LANES)
            x2 = x_ref[:, sl]
            x2b = x2.astype(BF16)
            ys, acol = [], []
            for h in (h0, h0 + 1):
                acol.append(jnp.broadcast_to(a[:, h:h + 1], (n, LANES)))
                seg = jnp.where(mask, acol[-1] - a_t[h:h + 1, :], NEG)
                w = cb * jnp.exp(seg) * dt_t[h:h + 1, :]
                ys.append(_dot(w.astype(BF16), x2b))
            a_rep = jnp.where(lo, acol[0], acol[1])
            dt_rep = jnp.where(lo, dt[:, h0:h0 + 1], dt[:, h0 + 1:h0 + 2])
            al_rep = jnp.where(lo1, a_last[:, h0:h0 + 1], a_last[:, h0 + 1:h0 + 2])
            y2 = jnp.where(lo, ys[0], ys[1]) + cs[:, j * LANES:(j + 1) * LANES] * jnp.exp(a_rep)
            xw.append((x2 * (jnp.exp(al_rep - a_rep) * dt_rep)).astype(BF16))
            dec.append(jnp.exp(al_rep))
            if final:
                y2 = y2 + yfw_ref[:, sl] + dsk_ref[:, sl] * x2
                yg.append(y2 * _silu(z_ref[:, sl]))
            else:
                y_ref[:, sl] = y2
        gs = slice(g * gw, (g + 1) * gw)
        st_ref[:, gs] = st_ref[:, gs] * jnp.concatenate(dec, 1) + _dot_tn(bg, jnp.concatenate(xw, 1))
        if final:
            yg = jnp.concatenate(yg, 1)
            yg = yg * lax.rsqrt(jnp.mean(yg * yg, -1, keepdims=True) + RMS_EPS) * nw_ref[:, gs]
            y_ref[:, gs] = yg.astype(y_ref.dtype)

    @pl.when(ci == pl.num_programs(0) - 1)
    def _():
        sfin_ref[...] = st_ref[...]


def ssd_scan(xbc, z, dt, dt_bias, a_log, s0, reverse, y_fw=None, d_skip=None, norm_w=None):
    n = xbc.shape[0]
    gn = SSM_GROUPS * SSM_STATE
    di = xbc.shape[1] - 2 * gn
    lc = SSM_CHUNK
    nc = n // lc
    final = y_fw is not None
    cidx = (lambda c: nc - 1 - c) if reverse else (lambda c: c)
    wide = lambda col: pl.BlockSpec((lc, di), lambda c: (cidx(c), col))
    vec = lambda w: pl.BlockSpec((1, w), lambda c: (0, 0))
    st = pl.BlockSpec((SSM_STATE, di), lambda c: (0, 0))
    in_specs = [wide(0),
                pl.BlockSpec((lc, gn), lambda c: (cidx(c), di // gn)),
                pl.BlockSpec((lc, gn), lambda c: (cidx(c), di // gn + 1)),
                pl.BlockSpec((lc, LANES), lambda c: (cidx(c), 0)),
                vec(LANES), vec(LANES), st]
    args = [xbc, xbc, xbc, dt, dt_bias, a_log, s0]
    if final:
        in_specs += [wide(0), wide(0), vec(di), vec(di)]
        args += [z, y_fw, d_skip, norm_w]
    return pl.pallas_call(
        functools.partial(_ssd_kernel, reverse=reverse, final=final),
        out_shape=(jax.ShapeDtypeStruct((n, di), BF16 if final else F32),
                   jax.ShapeDtypeStruct((SSM_STATE, di), F32)),
        grid=(nc,),
        in_specs=in_specs,
        out_specs=(wide(0), st),
        scratch_shapes=[pltpu.VMEM((SSM_STATE, di), F32)],
        compiler_params=_cp("arbitrary"),
        name="ssd_bw" if reverse else "ssd_fw",
    )(*args)


def _pad_cols(w, n):
    return jnp.pad(w, ((0, 0), (0, n - w.shape[1])))


def swa_mixer(h_ctx, h_lat, w_in, sink, cos2, sin2, with_ctx):
    w = w_in.astype(BF16)
    qkv_l = matmul(h_lat, w, F32, tn=1280)
    qkv_c = matmul(h_ctx, w, F32, tn=1280)
    o_lat, o_ctx = swa_attention(qkv_l, qkv_c, sink.astype(F32), cos2, sin2, with_ctx)
    return o_ctx, o_lat


def hgrn2_mixer(h_ctx, h_lat, w_in, lower_bound, norm_w, with_ctx):
    w = w_in.astype(BF16)
    hk = HG_HEADS * HG_KDIM
    lb = lower_bound.reshape(1, hk)
    nw = norm_w.reshape(1, hk)
    proj_c = matmul(h_ctx, w, F32, tn=1024)
    proj_l = matmul(h_lat, w, F32, tn=1024)
    zeros = jnp.zeros((HG_HEADS, HG_KDIM, HG_KDIM), F32)
    ofw_c, s_fw = hgrn2_scan(proj_c, lb, zeros, False)
    y_c, s_bw = hgrn2_scan(proj_c, lb, zeros, True, ofw_c, nw)
    ofw_l, _ = hgrn2_scan(proj_l, lb, s_fw, False)
    y_l, _ = hgrn2_scan(proj_l, lb, s_bw, True, ofw_l, nw)
    return (y_c if with_ctx else None), y_l


def mla_mixer(h_ctx, h_lat, w_in, q_norm, kv_norm, w_uq, w_ukv, cos2, sin2, with_ctx):
    t, nctx = h_lat.shape[0], h_ctx.shape[0]
    hh = MLA_HEADS
    ncol = MLA_Q_RANK + MLA_KV_RANK + LANES
    w1 = _pad_cols(w_in, ncol).astype(BF16)
    wq = w_uq.reshape(MLA_Q_RANK, hh, MLA_NOPE + MLA_ROPE)
    wq = jnp.pad(wq, ((0, 0), (0, 0), (0, LANES - MLA_ROPE))).reshape(MLA_Q_RANK, hh * (MLA_NOPE + LANES)).astype(BF16)
    wkv = w_ukv.reshape(MLA_KV_RANK, hh, MLA_NOPE + MLA_V)
    wkv = jnp.concatenate([wkv[:, :, :MLA_NOPE].reshape(MLA_KV_RANK, hh * MLA_NOPE),
                           wkv[:, :, MLA_NOPE:].reshape(MLA_KV_RANK, hh * MLA_V)], 1).astype(BF16)
    zpad = jnp.zeros((t, LANES - 2 * 32), F32)
    cos = jnp.concatenate([jnp.concatenate([cos2, zpad], 1), jnp.ones((nctx, LANES), F32)], 0)
    sin = jnp.concatenate([jnp.concatenate([sin2, zpad], 1), jnp.zeros((nctx, LANES), F32)], 0)
    h = jnp.concatenate([h_lat, h_ctx], 0)
    tm = 1280
    cq, ckv, kpe = mla_norm(matmul(h, w1, F32, tn=ncol, tm=tm), q_norm.reshape(1, -1), kv_norm.reshape(1, -1), cos, sin)
    q = matmul(cq, wq, F32, tn=1024, tm=tm)
    kv = matmul(ckv, wkv, BF16, tn=1024, tm=tm)
    o_ctx = mla_attention(q, cos, sin, kv, kpe, (t, nctx), (t, nctx), tq=nctx, tk=nctx) if with_ctx else None
    o_lat = mla_attention(q, cos, sin, kv, kpe, (0, t), (0, t + nctx), tq=1024, tk=1280)
    return o_ctx, o_lat


def mamba2_mixer(h_ctx, h_lat, w_in, conv_w, conv_b, dt_bias, a_log, d_skip, norm_w, with_ctx):
    di = norm_w.shape[0]
    gn = SSM_GROUPS * SSM_STATE
    heads = di // SSM_HEAD_DIM
    ncv = di + 2 * gn
    w_z, w_xbc, w_dt = (w_in[:, :di].astype(BF16), w_in[:, di:di + ncv].astype(BF16), w_in[:, di + ncv:].astype(BF16))
    dtb = dt_bias.reshape(1, 2 * heads).astype(F32)
    alog = a_log.reshape(1, 2 * heads).astype(F32)
    dsk = jnp.repeat(d_skip.astype(F32), SSM_HEAD_DIM).reshape(1, di)
    nw = norm_w.reshape(1, di)

    def prep(h):
        return (matmul_conv_silu(h, w_xbc, conv_w, conv_b), matmul(h, w_z, F32, tn=1024),
                matmul(h, w_dt, F32, tn=2 * heads))

    xbc_c, z_c, dt_c = prep(h_ctx)
    xbc_l, z_l, dt_l = prep(h_lat)
    zeros = jnp.zeros((SSM_STATE, di), F32)
    yfw_c, s_fw = ssd_scan(xbc_c, z_c, dt_c, dtb, alog, zeros, False)
    y_c, s_bw = ssd_scan(xbc_c, z_c, dt_c, dtb, alog, zeros, True, yfw_c, dsk, nw)
    yfw_l, _ = ssd_scan(xbc_l, z_l, dt_l, dtb, alog, s_fw, False)
    y_l, _ = ssd_scan(xbc_l, z_l, dt_l, dtb, alog, s_bw, True, yfw_l, dsk, nw)
    return (y_c if with_ctx else None), y_l


def kernel(x, c, ctx, c_ctx, ada_w, ada_b, ln_g, ln_b, ffn_up, ffn_conv_w, ffn_conv_b, ffn_down, swa_in, swa_sink, swa_out, hg_in, hg_lb_logits, hg_norm_w, hg_out, mla_in, mla_q_norm, mla_kv_norm, mla_uq, mla_ukv, mla_out, ssm_in, ssm_conv_w, ssm_conv_b, ssm_dt_bias, ssm_A_log, ssm_D, ssm_norm_w, ssm_out):
    batch, t, d = x.shape
    assert batch == 1
    depth = ada_w.shape[0]
    x_lat, x_ctx = x[0], ctx[0]

    probs = jax.nn.softmax(hg_lb_logits.astype(F32), axis=0)
    lower_bounds = jnp.cumsum(probs, axis=0) - probs[0]

    cvec = jnp.zeros((8, d), F32).at[0].set(c[0]).at[1].set(c_ctx)
    mods = ada_modulation(cvec, ada_w, ada_b)

    def mod(i, stream, j):
        return mods[i, stream:stream + 1, j * d:(j + 1) * d]

    cos2, sin2 = rope_tables(t)
    w_up, w_down = ffn_up.astype(BF16), ffn_down.astype(BF16)
    zero = jnp.zeros((1, d), F32)
    h_lat = modulate(x_lat, mod(0, 0, 0), mod(0, 0, 1))
    h_ctx = modulate(x_ctx, mod(0, 1, 0), mod(0, 1, 1))
    for i in range(depth):
        kind, j = i % N_MIXERS, i // N_MIXERS
        with_ctx = i < depth - 1
        if kind == 0:
            y_ctx, y_lat = swa_mixer(h_ctx, h_lat, swa_in[j], swa_sink[j], cos2, sin2, with_ctx)
            w_out = swa_out[j]
        elif kind == 1:
            y_ctx, y_lat = hgrn2_mixer(h_ctx, h_lat, hg_in[j], lower_bounds[i], hg_norm_w[j], with_ctx)
            w_out = hg_out[j]
        elif kind == 2:
            y_ctx, y_lat = mla_mixer(h_ctx, h_lat, mla_in[j], mla_q_norm[j], mla_kv_norm[j], mla_uq[j], mla_ukv[j],
                                     cos2, sin2, with_ctx)
            w_out = mla_out[j]
        else:
            y_ctx, y_lat = mamba2_mixer(h_ctx, h_lat, ssm_in[j], ssm_conv_w[j], ssm_conv_b[j], ssm_dt_bias[j],
                                        ssm_A_log[j], ssm_D[j], ssm_norm_w[j], with_ctx)
            w_out = ssm_out[j]
        w_out = w_out.astype(BF16)
        g0, b0, g1, b1 = ln_g[i, 0:1], ln_b[i, 0:1], ln_g[i, 1:2], ln_b[i, 1:2]
        streams = [(0, x_lat, y_lat)] + ([(1, x_ctx, y_ctx)] if with_ctx else [])
        outs = []
        for s, xs, ys in streams:
            xs, hf = matmul_ln(ys, w_out, xs, mod(i, s, 2), g0, b0, mod(i, s, 3), mod(i, s, 4))
            nsh, nsc = (mod(i + 1, s, 0), mod(i + 1, s, 1)) if i + 1 < depth else (zero, zero)
            outs.append(conv_ffn_ln(hf, w_up, ffn_conv_w[i], ffn_conv_b[i], w_down, i, xs, mod(i, s, 5), g1, b1, nsh, nsc))
        x_lat, h_lat = outs[0]
        if with_ctx:
            x_ctx, h_ctx = outs[1]
    return x_lat[None]
```

```python
import functools

import jax
import jax.numpy as jnp
from jax import lax
from jax.experimental import pallas as pl
from jax.experimental.pallas import tpu as pltpu

F32, BF16 = jnp.float32, jnp.bfloat16

DEPTH, N_MIXERS = 4, 4
GRID_W = 64
ALPHA = (2.0 * DEPTH) ** 0.25
LN_EPS, RMS_EPS, ROPE_BASE = 1e-5, 1e-6, 10000.0
SWA_HEADS, SWA_KV_HEADS, SWA_HEAD_DIM, SWA_BLOCK = 32, 4, 64, 128
HG_HEADS, HG_KDIM = 16, 128
MLA_HEADS, MLA_Q_RANK, MLA_KV_RANK, MLA_NOPE, MLA_ROPE, MLA_V = 16, 512, 512, 128, 64, 128
SSM_HEAD_DIM, SSM_STATE, SSM_GROUPS, SSM_CHUNK = 64, 128, 8, 128
HG_CHUNK = 64
LANES = 128
NEG = -1e30
LOG2E = 1.4426950408889634
VMEM_LIMIT = 56 * 1024 * 1024


def _cp(*sem):
    return pltpu.CompilerParams(dimension_semantics=sem, vmem_limit_bytes=VMEM_LIMIT)


def _dot(a, b):
    return jnp.dot(a, b, preferred_element_type=F32)


def _dot_nt(a, b):
    return lax.dot_general(a, b, (((1,), (1,)), ((), ())), preferred_element_type=F32)


def _dot_tn(a, b):
    return lax.dot_general(a, b, (((0,), (0,)), ((), ())), preferred_element_type=F32)


def _sigmoid(x):
    return 1.0 / (1.0 + jnp.exp(-x))


def _silu(x):
    return x * _sigmoid(x)


def _softplus(x):
    return jnp.maximum(x, 0.0) + jnp.log(1.0 + jnp.exp(-jnp.abs(x)))


def _ln_mod(z, g, b):
    mu = jnp.mean(z, -1, keepdims=True)
    d = z - mu
    var = jnp.mean(d * d, -1, keepdims=True)
    return d * lax.rsqrt(var + LN_EPS) * g + b


def _shift_rows(x, prev_row, next_row):
    n = x.shape[0]
    row = lax.broadcasted_iota(jnp.int32, (n, 1), 0)
    dn = jnp.where(row == 0, prev_row, pltpu.roll(x, 1, 0))
    up = jnp.where(row == n - 1, next_row, pltpu.roll(x, n - 1, 0))
    return dn, up


def _cumsum_rows(x, reverse):
    n = x.shape[0]
    row = lax.broadcasted_iota(jnp.int32, (n, 1), 0)
    s = 1
    while s < n:
        if reverse:
            x = x + jnp.where(row < n - s, pltpu.roll(x, n - s, 0), 0.0)
        else:
            x = x + jnp.where(row >= s, pltpu.roll(x, s, 0), 0.0)
        s *= 2
    return x


def _ada_kernel(c_ref, w_ref, b_ref, o_ref):
    a = _silu(c_ref[...]).astype(BF16)
    o_ref[0] = _dot(a, w_ref[0].astype(BF16)) + b_ref[0]


def ada_modulation(cvec, ada_w, ada_b):
    depth, d, n = ada_w.shape
    tn = 1024
    return pl.pallas_call(
        _ada_kernel,
        out_shape=jax.ShapeDtypeStruct((depth, 8, n), F32),
        grid=(depth, n // tn),
        in_specs=[pl.BlockSpec((8, d), lambda i, j: (0, 0)),
                  pl.BlockSpec((1, d, tn), lambda i, j: (i, 0, j)),
                  pl.BlockSpec((1, 1, tn), lambda i, j: (i, 0, j))],
        out_specs=pl.BlockSpec((1, 8, tn), lambda i, j: (i, 0, j)),
        compiler_params=_cp("parallel", "parallel"),
        name="ada_modulation",
    )(cvec, ada_w, ada_b.reshape(depth, 1, n))


def _modulate_kernel(x_ref, sh_ref, sc_ref, o_ref):
    o_ref[...] = (x_ref[...] * (1.0 + sc_ref[...]) + sh_ref[...]).astype(o_ref.dtype)


def modulate(x, shift, scale):
    m, d = x.shape
    tm = min(512, m)
    vec = pl.BlockSpec((1, d), lambda i: (0, 0))
    return pl.pallas_call(
        _modulate_kernel,
        out_shape=jax.ShapeDtypeStruct((m, d), BF16),
        grid=(m // tm,),
        in_specs=[pl.BlockSpec((tm, d), lambda i: (i, 0)), vec, vec],
        out_specs=pl.BlockSpec((tm, d), lambda i: (i, 0)),
        compiler_params=_cp("parallel"),
        name="modulate",
    )(x, shift, scale)


def _mm_kernel(a_ref, w_ref, o_ref):
    o_ref[...] = _dot(a_ref[...], w_ref[...]).astype(o_ref.dtype)


def matmul(a, w, out_dtype, tn, tm=1024):
    m, k = a.shape
    n = w.shape[1]
    tm = min(tm, m)
    assert m % tm == 0 and n % tn == 0
    return pl.pallas_call(
        _mm_kernel,
        out_shape=jax.ShapeDtypeStruct((m, n), out_dtype),
        grid=(m // tm, n // tn),
        in_specs=[pl.BlockSpec((tm, k), lambda i, j: (i, 0)),
                  pl.BlockSpec((k, tn), lambda i, j: (0, j))],
        out_specs=pl.BlockSpec((tm, tn), lambda i, j: (i, j)),
        compiler_params=_cp("parallel", "parallel"),
        name="matmul",
    )(a, w)


def _mm_ln_kernel(a_ref, w_ref, x_ref, gate_ref, g_ref, b_ref, sh_ref, sc_ref, xo_ref, ho_ref, acc_ref):
    k = pl.program_id(1)

    @pl.when(k == 0)
    def _():
        acc_ref[...] = jnp.zeros_like(acc_ref)

    acc_ref[...] += _dot(a_ref[...], w_ref[...])

    @pl.when(k == pl.num_programs(1) - 1)
    def _():
        xn = _ln_mod(ALPHA * x_ref[...] + gate_ref[...] * acc_ref[...], g_ref[...], b_ref[...])
        xo_ref[...] = xn
        ho_ref[...] = (xn * (1.0 + sc_ref[...]) + sh_ref[...]).astype(ho_ref.dtype)


def matmul_ln(a, w, x, gate, g, b, shift, scale):
    m, k = a.shape
    d = w.shape[1]
    tm, tk = min(512, m), 2048
    vec = pl.BlockSpec((1, d), lambda i, kk: (0, 0))
    row = pl.BlockSpec((tm, d), lambda i, kk: (i, 0))
    return pl.pallas_call(
        _mm_ln_kernel,
        out_shape=(jax.ShapeDtypeStruct((m, d), F32), jax.ShapeDtypeStruct((m, d), BF16)),
        grid=(m // tm, k // tk),
        in_specs=[pl.BlockSpec((tm, tk), lambda i, kk: (i, kk)),
                  pl.BlockSpec((tk, d), lambda i, kk: (kk, 0)),
                  row, vec, vec, vec, vec, vec],
        out_specs=(row, row),
        scratch_shapes=[pltpu.VMEM((tm, d), F32)],
        compiler_params=_cp("parallel", "arbitrary"),
        name="matmul_ln",
    )(a, w, x, gate, g, b, shift, scale)


def _ffn_kernel(h_ref, hp_ref, hn_ref, wa_ref, wv_ref, cw_ref, cb_ref, wd_ref, x_ref, gate_ref, g_ref, b_ref,
                sh_ref, sc_ref, xo_ref, ho_ref):
    i, f = pl.program_id(0), pl.program_id(1)
    acc_ref = xo_ref

    @pl.when(f == 0)
    def _():
        acc_ref[...] = jnp.zeros_like(acc_ref)

    wa = wa_ref[...]
    nrow = hp_ref.shape[0]
    a_prev = _dot(hp_ref[...], wa)[nrow - 1:nrow, :]
    a_next = _dot(hn_ref[...], wa)[0:1, :]
    a_prev = jnp.where(i > 0, a_prev, 0.0)
    a_next = jnp.where(i < pl.num_programs(0) - 1, a_next, 0.0)
    cw = cw_ref[...]
    a = _dot(h_ref[...], wa)
    v = _dot(h_ref[...], wv_ref[...])
    dn, up = _shift_rows(a, a_prev, a_next)
    cv = cw[0:1, :] * dn + cw[1:2, :] * a + cw[2:3, :] * up + cb_ref[...]
    u = (_silu(cv) * v).astype(BF16)
    acc_ref[...] += _dot(u, wd_ref[...])

    @pl.when(f == pl.num_programs(1) - 1)
    def _():
        xn = _ln_mod(ALPHA * x_ref[...] + gate_ref[...] * acc_ref[...], g_ref[...], b_ref[...])
        xo_ref[...] = xn
        ho_ref[...] = (xn * (1.0 + sc_ref[...]) + sh_ref[...]).astype(ho_ref.dtype)


def conv_ffn_ln(h, w_up, conv_w, conv_b, w_down, layer, x, gate, g, b, shift, scale):
    m, d = h.shape
    f = w_down.shape[1]
    tm, tf, halo = min(1024, m), 512, 16
    nf = f // tf
    nh = m // halo
    vec = pl.BlockSpec((1, d), lambda i, j: (0, 0))
    row = pl.BlockSpec((tm, d), lambda i, j: (i, 0))
    once = pl.BlockSpec((tm, d), lambda i, j: (i, 0), pipeline_mode=pl.Buffered(1))
    return pl.pallas_call(
        _ffn_kernel,
        out_shape=(jax.ShapeDtypeStruct((m, d), F32), jax.ShapeDtypeStruct((m, d), BF16)),
        grid=(m // tm, nf),
        in_specs=[row,
                  pl.BlockSpec((halo, d), lambda i, j: (jnp.maximum(i * (tm // halo) - 1, 0), 0)),
                  pl.BlockSpec((halo, d), lambda i, j: (jnp.minimum((i + 1) * (tm // halo), nh - 1), 0)),
                  pl.BlockSpec((None, d, tf), lambda i, j: (layer, 0, j)),
                  pl.BlockSpec((None, d, tf), lambda i, j: (layer, 0, nf + j)),
                  pl.BlockSpec((3, tf), lambda i, j: (0, j)),
                  pl.BlockSpec((1, tf), lambda i, j: (0, j)),
                  pl.BlockSpec((None, tf, d), lambda i, j: (layer, j, 0)),
                  once, vec, vec, vec, vec, vec],
        out_specs=(once, once),
        compiler_params=_cp("parallel", "arbitrary"),
        name="conv_ffn_ln",
    )(h, h, h, w_up, w_up, conv_w, conv_b.reshape(1, f), w_down, x, gate, g, b, shift, scale)


def rope_tables(n_tokens):
    rows = n_tokens // GRID_W
    row = jnp.repeat(jnp.arange(rows, dtype=F32), GRID_W)
    col = jnp.tile(jnp.arange(GRID_W, dtype=F32), rows)
    n_axis = 16
    inv_freq = ROPE_BASE ** (-jnp.arange(n_axis, dtype=F32) / n_axis)
    ang = jnp.concatenate([row[:, None] * inv_freq, col[:, None] * inv_freq], -1)
    cos, sin = jnp.cos(ang), jnp.sin(ang)
    return jnp.concatenate([cos, cos], -1), jnp.concatenate([-sin, sin], -1)


def _rope2(x, cos, sin):
    lane = lax.broadcasted_iota(jnp.int32, x.shape, 1)
    rot = jnp.where((lane & 63) < 32, pltpu.roll(x, 96, 1), pltpu.roll(x, 32, 1))
    return x * cos + rot * sin


def _split_halves(x, upper):
    lane = lax.broadcasted_iota(jnp.int32, x.shape, 1)
    if upper:
        hi = jnp.where(lane >= 64, x, 0.0)
        return pltpu.roll(hi, 64, 1), hi
    lo = jnp.where(lane < 64, x, 0.0)
    return lo, pltpu.roll(lo, 64, 1)


def _block_diag(pieces, upper):
    halves = [_split_halves(x, upper) for x in pieces]
    return jnp.concatenate([h[0] for h in halves] + [h[1] for h in halves], 0).astype(BF16)


def _swa_groups(sink_ref, q_ref, keys, vals, bias, o_ref, rope):
    blk = q_ref.shape[0]
    hpg = SWA_HEADS // SWA_KV_HEADS // 2
    rowp = lax.shift_right_logical(lax.broadcasted_iota(jnp.int32, (hpg * blk, 1), 0), blk.bit_length() - 1)
    lane = lax.broadcasted_iota(jnp.int32, (hpg * blk, LANES), 1)
    for g in range(SWA_KV_HEADS):
        qs = []
        for pp in range(hpg):
            q2 = q_ref[:, (g * hpg + pp) * LANES:(g * hpg + pp + 1) * LANES]
            if rope is not None:
                q2 = _rope2(q2, *rope)
            qs.append(q2 * (SWA_HEAD_DIM ** -0.5 * LOG2E))
        s = _dot_nt(jnp.concatenate(qs, 0).astype(BF16), keys[g])
        nk = s.shape[1] // 2
        ps, dens = [], []
        for half in range(2):
            sh = s[:, half * nk:(half + 1) * nk]
            if bias is not None:
                sh = sh + bias
            sink = jnp.zeros((hpg * blk, 1), F32)
            for pp in range(hpg):
                sink = jnp.where(rowp == pp, sink_ref[2 * (g * hpg + pp) + half] * LOG2E, sink)
            m = jnp.maximum(jnp.max(sh, -1, keepdims=True), sink)
            p = jnp.exp2(sh - m)
            dens.append(jnp.sum(p, -1, keepdims=True) + jnp.exp2(sink - m))
            ps.append(p.astype(BF16))
        o = _dot(jnp.concatenate(ps, 1), vals[g]) / jnp.where(lane < 64, dens[0], dens[1])
        for pp in range(hpg):
            o_ref[:, (g * hpg + pp) * LANES:(g * hpg + pp + 1) * LANES] = o[pp * blk:(pp + 1) * blk].astype(o_ref.dtype)


def _swa_lat_kernel(sink_ref, q_ref, kp_ref, kc_ref, kn_ref, vp_ref, vc_ref, vn_ref, kx_ref, vx_ref,
                    cq_ref, sq_ref, cp_ref, sp_ref, cn_ref, sn_ref, o_ref):
    i, nb = pl.program_id(0), pl.num_programs(0)
    blk = q_ref.shape[0]
    nctx = kx_ref.shape[0]
    keys, vals = [], []
    for g in range(SWA_KV_HEADS):
        sl = slice((g // 2) * LANES, (g // 2 + 1) * LANES)
        up = g % 2 == 1
        keys.append(_block_diag([_rope2(kp_ref[:, sl], cp_ref[...], sp_ref[...]),
                                 _rope2(kc_ref[:, sl], cq_ref[...], sq_ref[...]),
                                 _rope2(kn_ref[:, sl], cn_ref[...], sn_ref[...]), kx_ref[:, sl]], up))
        vals.append(_block_diag([vp_ref[:, sl], vc_ref[:, sl], vn_ref[:, sl], vx_ref[:, sl]], up))
    nk = 3 * blk + nctx
    rows = (SWA_HEADS // SWA_KV_HEADS // 2) * blk
    r = lax.broadcasted_iota(jnp.int32, (rows, nk), 0) & (blk - 1)
    c = lax.broadcasted_iota(jnp.int32, (rows, nk), 1)
    has_prev = jnp.where(i > 0, 0.0, NEG)
    has_next = jnp.where(i < nb - 1, 0.0, NEG)
    bias = jnp.where(c < blk, jnp.where(c >= r, has_prev, NEG),
                     jnp.where(c < 2 * blk, 0.0,
                               jnp.where(c < 3 * blk, jnp.where(c - 2 * blk <= r, has_next, NEG), 0.0)))
    _swa_groups(sink_ref, q_ref, keys, vals, bias, o_ref, (cq_ref[...], sq_ref[...]))


def _swa_ctx_kernel(sink_ref, q_ref, kx_ref, vx_ref, o_ref):
    keys, vals = [], []
    for g in range(SWA_KV_HEADS):
        sl = slice((g // 2) * LANES, (g // 2 + 1) * LANES)
        keys.append(_block_diag([kx_ref[:, sl]], g % 2 == 1))
        vals.append(_block_diag([vx_ref[:, sl]], g % 2 == 1))
    _swa_groups(sink_ref, q_ref, keys, vals, None, o_ref, None)


def swa_attention(qkv_lat, qkv_ctx, sink, cos2, sin2, with_ctx):
    t, nctx = qkv_lat.shape[0], qkv_ctx.shape[0]
    blk, dq = SWA_BLOCK, SWA_HEADS * SWA_HEAD_DIM
    dkv = SWA_KV_HEADS * SWA_HEAD_DIM
    kcol, vcol = dq // dkv, dq // dkv + 1
    nb = t // blk
    cos = jnp.concatenate([cos2, cos2], -1)
    sin = jnp.concatenate([sin2, sin2], -1)
    smem = pl.BlockSpec(memory_space=pltpu.SMEM)
    prev = lambda i: jnp.maximum(i - 1, 0)
    nxt = lambda i: jnp.minimum(i + 1, nb - 1)
    tab = lambda f: pl.BlockSpec((blk, LANES), lambda i: (f(i), 0))
    kv = lambda f, col: pl.BlockSpec((blk, dkv), lambda i: (f(i), col))
    same = lambda i: i
    o_lat = pl.pallas_call(
        _swa_lat_kernel,
        out_shape=jax.ShapeDtypeStruct((t, dq), BF16),
        grid=(nb,),
        in_specs=[smem, pl.BlockSpec((blk, dq), lambda i: (i, 0)),
                  kv(prev, kcol), kv(same, kcol), kv(nxt, kcol), kv(prev, vcol), kv(same, vcol), kv(nxt, vcol),
                  pl.BlockSpec((nctx, dkv), lambda i: (0, kcol)), pl.BlockSpec((nctx, dkv), lambda i: (0, vcol)),
                  tab(same), tab(same), tab(prev), tab(prev), tab(nxt), tab(nxt)],
        out_specs=pl.BlockSpec((blk, dq), lambda i: (i, 0)),
        compiler_params=_cp("parallel"),
        name="swa_lat",
    )(sink, qkv_lat, qkv_lat, qkv_lat, qkv_lat, qkv_lat, qkv_lat, qkv_lat, qkv_ctx, qkv_ctx,
      cos, sin, cos, sin, cos, sin)
    if not with_ctx:
        return o_lat, None
    o_ctx = pl.pallas_call(
        _swa_ctx_kernel,
        out_shape=jax.ShapeDtypeStruct((nctx, dq), BF16),
        grid=(nctx // blk,),
        in_specs=[smem, pl.BlockSpec((blk, dq), lambda i: (i, 0)),
                  pl.BlockSpec((nctx, dkv), lambda i: (0, kcol)), pl.BlockSpec((nctx, dkv), lambda i: (0, vcol))],
        out_specs=pl.BlockSpec((blk, dq), lambda i: (i, 0)),
        compiler_params=_cp("parallel"),
        name="swa_ctx",
    )(sink, qkv_ctx, qkv_ctx, qkv_ctx)
    return o_lat, o_ctx


def _hg_kernel(*refs, reverse, final):
    if final:
        q_ref, f_ref, v_ref, lb_ref, s0_ref, g_ref, ofw_ref, nw_ref, o_ref, sfin_ref, st_ref = refs
    else:
        q_ref, f_ref, v_ref, lb_ref, s0_ref, o_ref, sfin_ref, st_ref = refs
    c = pl.program_id(0)

    @pl.when(c == 0)
    def _():
        st_ref[...] = s0_ref[...]

    n = HG_CHUNK
    kd = HG_KDIM
    lb = lb_ref[...]
    ti = lax.broadcasted_iota(jnp.int32, (n, n), 0)
    si = lax.broadcasted_iota(jnp.int32, (n, n), 1)
    mask = (si >= ti) if reverse else (si <= ti)
    nsub = q_ref.shape[0] // n
    for sc in (range(nsub - 1, -1, -1) if reverse else range(nsub)):
        rows = slice(sc * n, (sc + 1) * n)
        q = _silu(q_ref[rows, :]) * kd ** -0.5
        sg = _sigmoid(f_ref[rows, :])
        lf = jnp.log(lb + (1.0 - lb) * sg)
        k = (1.0 - lb) * (1.0 - sg)
        v = v_ref[rows, :].astype(BF16)
        b = _cumsum_rows(lf, reverse)
        mid = n // 2 if reverse else n // 2 - 1
        last = 0 if reverse else n - 1
        r = b[mid:mid + 1, :]
        b_last = b[last:last + 1, :]
        qe = (q * jnp.exp(b - r)).astype(BF16)
        ke = (k * jnp.exp(r - b)).astype(BF16)
        qs = (q * jnp.exp(b)).astype(BF16)
        kl = (k * jnp.exp(b_last - b)).astype(BF16)
        dec = jnp.exp(b_last)
        for h in range(q_ref.shape[1] // kd):
            sl = slice(h * kd, (h + 1) * kd)
            att = jnp.where(mask, _dot_nt(qe[:, sl], ke[:, sl]), 0.0).astype(BF16)
            st = st_ref[h]
            o = _dot(att, v[:, sl]) + _dot_nt(qs[:, sl], st.astype(BF16))
            st_ref[h] = st * dec[:, sl] + _dot_tn(v[:, sl], kl[:, sl])
            if final:
                o = o + ofw_ref[rows, sl]
                o = o * lax.rsqrt(jnp.mean(o * o, -1, keepdims=True) + RMS_EPS) * nw_ref[:, sl]
                o_ref[rows, sl] = (o * _silu(g_ref[rows, sl])).astype(o_ref.dtype)
            else:
                o_ref[rows, sl] = o

    @pl.when(c == pl.num_programs(0) - 1)
    def _():
        sfin_ref[...] = st_ref[...]


def hgrn2_scan(proj, lb, s0, reverse, o_fw=None, norm_w=None):
    n = proj.shape[0]
    hk = HG_HEADS * HG_KDIM
    lc = next(m * HG_CHUNK for m in (4, 2, 1) if n % (m * HG_CHUNK) == 0)
    nc = n // lc
    final = o_fw is not None
    cidx = (lambda c: nc - 1 - c) if reverse else (lambda c: c)
    blk = lambda col: pl.BlockSpec((lc, hk), lambda c: (cidx(c), col))
    vec = pl.BlockSpec((1, hk), lambda c: (0, 0))
    st = pl.BlockSpec((HG_HEADS, HG_KDIM, HG_KDIM), lambda c: (0, 0, 0))
    in_specs = [blk(0), blk(2 if reverse else 1), blk(3), vec, st]
    args = [proj, proj, proj, lb, s0]
    if final:
        in_specs += [blk(4), blk(0), vec]
        args += [proj, o_fw, norm_w]
    return pl.pallas_call(
        functools.partial(_hg_kernel, reverse=reverse, final=final),
        out_shape=(jax.ShapeDtypeStruct((n, hk), BF16 if final else F32),
                   jax.ShapeDtypeStruct((HG_HEADS, HG_KDIM, HG_KDIM), F32)),
        grid=(nc,),
        in_specs=in_specs,
        out_specs=(blk(0), st),
        scratch_shapes=[pltpu.VMEM((HG_HEADS, HG_KDIM, HG_KDIM), F32)],
        compiler_params=_cp("arbitrary"),
        name="hgrn2_bw" if reverse else "hgrn2_fw",
    )(*args)


def _mla_norm_kernel(p_ref, qn_ref, kvn_ref, cos_ref, sin_ref, cq_ref, ckv_ref, kpe_ref):
    cq = p_ref[:, :MLA_Q_RANK]
    ckv = p_ref[:, MLA_Q_RANK:MLA_Q_RANK + MLA_KV_RANK]
    kp = p_ref[:, MLA_Q_RANK + MLA_KV_RANK:]
    cq_ref[...] = (cq * lax.rsqrt(jnp.mean(cq * cq, -1, keepdims=True) + RMS_EPS) * qn_ref[...]).astype(BF16)
    ckv_ref[...] = (ckv * lax.rsqrt(jnp.mean(ckv * ckv, -1, keepdims=True) + RMS_EPS) * kvn_ref[...]).astype(BF16)
    rot = pltpu.roll(kp, 96, 1) + pltpu.roll(kp, 32, 1)
    kpe_ref[...] = (kp * cos_ref[...] + rot * sin_ref[...]).astype(BF16)


def mla_norm(proj, q_norm, kv_norm, cos, sin):
    m = proj.shape[0]
    tm = 640 if m % 640 == 0 else 128
    assert m % tm == 0
    return pl.pallas_call(
        _mla_norm_kernel,
        out_shape=(jax.ShapeDtypeStruct((m, MLA_Q_RANK), BF16), jax.ShapeDtypeStruct((m, MLA_KV_RANK), BF16),
                   jax.ShapeDtypeStruct((m, LANES), BF16)),
        grid=(m // tm,),
        in_specs=[pl.BlockSpec((tm, proj.shape[1]), lambda i: (i, 0)),
                  pl.BlockSpec((1, MLA_Q_RANK), lambda i: (0, 0)), pl.BlockSpec((1, MLA_KV_RANK), lambda i: (0, 0)),
                  pl.BlockSpec((tm, LANES), lambda i: (i, 0)), pl.BlockSpec((tm, LANES), lambda i: (i, 0))],
        out_specs=(pl.BlockSpec((tm, MLA_Q_RANK), lambda i: (i, 0)), pl.BlockSpec((tm, MLA_KV_RANK), lambda i: (i, 0)),
                   pl.BlockSpec((tm, LANES), lambda i: (i, 0))),
        compiler_params=_cp("parallel"),
        name="mla_norm",
    )(proj, q_norm, kv_norm, cos, sin)


def _mla_attn_kernel(q_ref, cos_ref, sin_ref, kn_ref, kpe_ref, v_ref, o_ref,
                     q_scr, s0_scr, s1_scr, p0_scr, p1_scr, a0_scr, a1_scr, m_scr, acc_scr, *, tk):
    n = kn_ref.shape[0] // tk
    s_scr, p_scr, a_scr = (s0_scr, s1_scr), (p0_scr, p1_scr), (a0_scr, a1_scr)

    qp = q_ref[:, MLA_NOPE:]
    rot = pltpu.roll(qp, 96, 1) + pltpu.roll(qp, 32, 1)
    qp = qp * cos_ref[...] + rot * sin_ref[...]
    scale = (MLA_NOPE + MLA_ROPE) ** -0.5 * LOG2E
    q_scr[...] = (jnp.concatenate([q_ref[:, :MLA_NOPE], qp], 1) * scale).astype(BF16)
    m_scr[...] = jnp.full_like(m_scr, NEG)
    acc_scr[...] = jnp.zeros_like(acc_scr)
    ones = jnp.ones((tk, LANES), BF16)

    def rows(j):
        return pl.ds(j * tk, tk) if isinstance(j, int) else pl.ds(pl.multiple_of(j * tk, tk), tk)

    def scores(j, slot):
        k = jnp.concatenate([kn_ref[rows(j), :], kpe_ref[rows(j), :]], 1)
        s_scr[slot][...] = _dot_nt(q_scr[...], k)

    def softmax(slot):
        s = s_scr[slot][...]
        m_old = m_scr[...]
        m_new = jnp.maximum(m_old, jnp.max(s, -1, keepdims=True))
        p_scr[slot][...] = jnp.exp2(s - m_new).astype(BF16)
        a_scr[slot][...] = jnp.exp2(m_old - m_new)
        m_scr[...] = m_new

    def weighted(j, slot):
        v1 = jnp.concatenate([v_ref[rows(j), :], ones], 1)
        acc_scr[...] = a_scr[slot][...] * acc_scr[...] + _dot(p_scr[slot][...], v1)

    def steady(t, a, b):
        scores(t + 1, a)
        softmax(b)
        weighted(t - 1, a)

    scores(0, 0)
    if n > 1:
        scores(1, 1)
        softmax(0)
        pairs = (n - 2) // 2

        def pair(i, carry):
            steady(1 + 2 * i, 0, 1)
            steady(2 + 2 * i, 1, 0)
            return carry

        if pairs > 0:
            lax.fori_loop(0, pairs, pair, 0)
        if (n - 2) % 2:
            steady(n - 2, 0, 1)
        weighted(n - 2, (n - 2) % 2)
    softmax((n - 1) % 2)
    weighted(n - 1, (n - 1) % 2)
    o_ref[...] = (acc_scr[:, :MLA_V] / acc_scr[:, MLA_V:]).astype(o_ref.dtype)


def mla_attention(q, cos, sin, kv, kpe, q_rows, k_rows, tq, tk):
    (q0, nq), (k0, nk) = q_rows, k_rows
    hq = MLA_NOPE + LANES
    assert nk % tk == 0 and nq % tq == 0 and q0 % tq == 0 and k0 % nk == 0
    qb, kb = q0 // tq, k0 // nk
    return pl.pallas_call(
        functools.partial(_mla_attn_kernel, tk=tk),
        out_shape=jax.ShapeDtypeStruct((nq, MLA_HEADS * MLA_V), BF16),
        grid=(MLA_HEADS, nq // tq),
        in_specs=[pl.BlockSpec((tq, hq), lambda h, i: (qb + i, h)),
                  pl.BlockSpec((tq, LANES), lambda h, i: (qb + i, 0)),
                  pl.BlockSpec((tq, LANES), lambda h, i: (qb + i, 0)),
                  pl.BlockSpec((nk, MLA_NOPE), lambda h, i: (kb, h)),
                  pl.BlockSpec((nk, LANES), lambda h, i: (kb, 0)),
                  pl.BlockSpec((nk, MLA_V), lambda h, i: (kb, MLA_HEADS + h))],
        out_specs=pl.BlockSpec((tq, MLA_V), lambda h, i: (i, h)),
        scratch_shapes=[pltpu.VMEM((tq, hq), BF16),
                        pltpu.VMEM((tq, tk), F32), pltpu.VMEM((tq, tk), F32),
                        pltpu.VMEM((tq, tk), BF16), pltpu.VMEM((tq, tk), BF16),
                        pltpu.VMEM((tq, 1), F32), pltpu.VMEM((tq, 1), F32), pltpu.VMEM((tq, 1), F32),
                        pltpu.VMEM((tq, MLA_V + LANES), F32)],
        compiler_params=_cp("parallel", "parallel"),
        name="mla_attention",
    )(q, cos, sin, kv, kpe, kv)


def _mm_conv_silu_kernel(h_ref, hp_ref, hn_ref, w_ref, cw_ref, cb_ref, o_ref):
    i = pl.program_id(0)
    w = w_ref[...]
    nrow = hp_ref.shape[0]
    a = _dot(h_ref[...], w)
    a_prev = jnp.where(i > 0, _dot(hp_ref[...], w)[nrow - 1:nrow, :], 0.0)
    a_next = jnp.where(i < pl.num_programs(0) - 1, _dot(hn_ref[...], w)[0:1, :], 0.0)
    dn, up = _shift_rows(a, a_prev, a_next)
    cw = cw_ref[...]
    o_ref[...] = _silu(cw[0:1, :] * dn + cw[1:2, :] * a + cw[2:3, :] * up + cb_ref[...])


def matmul_conv_silu(h, w, conv_w, conv_b):
    n, k = h.shape
    ncol = w.shape[1]
    tm, tn, halo = min(1024, n), 1024, 16
    nh = n // halo
    return pl.pallas_call(
        _mm_conv_silu_kernel,
        out_shape=jax.ShapeDtypeStruct((n, ncol), F32),
        grid=(n // tm, ncol // tn),
        in_specs=[pl.BlockSpec((tm, k), lambda i, j: (i, 0)),
                  pl.BlockSpec((halo, k), lambda i, j: (jnp.maximum(i * (tm // halo) - 1, 0), 0)),
                  pl.BlockSpec((halo, k), lambda i, j: (jnp.minimum((i + 1) * (tm // halo), nh - 1), 0)),
                  pl.BlockSpec((k, tn), lambda i, j: (0, j)),
                  pl.BlockSpec((3, tn), lambda i, j: (0, j)),
                  pl.BlockSpec((1, tn), lambda i, j: (0, j))],
        out_specs=pl.BlockSpec((tm, tn), lambda i, j: (i, j)),
        compiler_params=_cp("parallel", "parallel"),
        name="matmul_conv_silu",
    )(h, h, h, w, conv_w, conv_b.reshape(1, ncol))


def _ssd_kernel(*refs, reverse, final):
    if final:
        (x_ref, b_ref, c_ref, dt_ref, dtb_ref, alog_ref, s0_ref, z_ref, yfw_ref, dsk_ref, nw_ref,
         y_ref, sfin_ref, st_ref) = refs
    else:
        x_ref, b_ref, c_ref, dt_ref, dtb_ref, alog_ref, s0_ref, y_ref, sfin_ref, st_ref = refs
    ci = pl.program_id(0)

    @pl.when(ci == 0)
    def _():
        st_ref[...] = s0_ref[...]

    n = x_ref.shape[0]
    p, ns = SSM_HEAD_DIM, SSM_STATE
    heads = x_ref.shape[1] // p
    hpg = heads // SSM_GROUPS
    gw = hpg * p
    dt = _softplus(dt_ref[...] + dtb_ref[...])
    a = _cumsum_rows(dt * -jnp.exp(alog_ref[...]), reverse)
    a_t, dt_t = a.T, dt.T
    last = 0 if reverse else n - 1
    a_last = a[last:last + 1, :]
    ti = lax.broadcasted_iota(jnp.int32, (n, n), 0)
    si = lax.broadcasted_iota(jnp.int32, (n, n), 1)
    mask = (si >= ti) if reverse else (si <= ti)
    lane = lax.broadcasted_iota(jnp.int32, (n, LANES), 1)
    lo = lane < p
    lo1 = lax.broadcasted_iota(jnp.int32, (1, LANES), 1) < p
    hoff = heads if reverse else 0
    for g in range(SSM_GROUPS):
        bg = b_ref[:, g * ns:(g + 1) * ns].astype(BF16)
        cg = c_ref[:, g * ns:(g + 1) * ns].astype(BF16)
        cb = _dot_nt(cg, bg)
        cs = _dot(cg, st_ref[:, g * gw:(g + 1) * gw].astype(BF16))
        xw, dec, yg = [], [], []
        for j in range(hpg // 2):
            h0 = hoff + g * hpg + 2 * j
            sl = slice(g * gw + j * LANES, g * gw + (j + 1) * LANES)
            x2 = x_ref[:, sl]
            x2b = x2.astype(BF16)
            ys, acol = [], []
            for h in (h0, h0 + 1):
                acol.append(jnp.broadcast_to(a[:, h:h + 1], (n, LANES)))
                seg = jnp.where(mask, acol[-1] - a_t[h:h + 1, :], NEG)
                w = cb * jnp.exp(seg) * dt_t[h:h + 1, :]
                ys.append(_dot(w.astype(BF16), x2b))
            a_rep = jnp.where(lo, acol[0], acol[1])
            dt_rep = jnp.where(lo, dt[:, h0:h0 + 1], dt[:, h0 + 1:h0 + 2])
            al_rep = jnp.where(lo1, a_last[:, h0:h0 + 1], a_last[:, h0 + 1:h0 + 2])
            y2 = jnp.where(lo, ys[0], ys[1]) + cs[:, j * LANES:(j + 1) * LANES] * jnp.exp(a_rep)
            xw.append((x2 * (jnp.exp(al_rep - a_rep) * dt_rep)).astype(BF16))
            dec.append(jnp.exp(al_rep))
            if final:
                y2 = y2 + yfw_ref[:, sl] + dsk_ref[:, sl] * x2
                yg.append(y2 * _silu(z_ref[:, sl]))
            else:
                y_ref[:, sl] = y2
        gs = slice(g * gw, (g + 1) * gw)
        st_ref[:, gs] = st_ref[:, gs] * jnp.concatenate(dec, 1) + _dot_tn(bg, jnp.concatenate(xw, 1))
        if final:
            yg = jnp.concatenate(yg, 1)
            yg = yg * lax.rsqrt(jnp.mean(yg * yg, -1, keepdims=True) + RMS_EPS) * nw_ref[:, gs]
            y_ref[:, gs] = yg.astype(y_ref.dtype)

    @pl.when(ci == pl.num_programs(0) - 1)
    def _():
        sfin_ref[...] = st_ref[...]


def ssd_scan(xbc, z, dt, dt_bias, a_log, s0, reverse, y_fw=None, d_skip=None, norm_w=None):
    n = xbc.shape[0]
    gn = SSM_GROUPS * SSM_STATE
    di = xbc.shape[1] - 2 * gn
    lc = SSM_CHUNK
    nc = n // lc
    final = y_fw is not None
    cidx = (lambda c: nc - 1 - c) if reverse else (lambda c: c)
    wide = lambda col: pl.BlockSpec((lc, di), lambda c: (cidx(c), col))
    vec = lambda w: pl.BlockSpec((1, w), lambda c: (0, 0))
    st = pl.BlockSpec((SSM_STATE, di), lambda c: (0, 0))
    in_specs = [wide(0),
                pl.BlockSpec((lc, gn), lambda c: (cidx(c), di // gn)),
                pl.BlockSpec((lc, gn), lambda c: (cidx(c), di // gn + 1)),
                pl.BlockSpec((lc, LANES), lambda c: (cidx(c), 0)),
                vec(LANES), vec(LANES), st]
    args = [xbc, xbc, xbc, dt, dt_bias, a_log, s0]
    if final:
        in_specs += [wide(0), wide(0), vec(di), vec(di)]
        args += [z, y_fw, d_skip, norm_w]
    return pl.pallas_call(
        functools.partial(_ssd_kernel, reverse=reverse, final=final),
        out_shape=(jax.ShapeDtypeStruct((n, di), BF16 if final else F32),
                   jax.ShapeDtypeStruct((SSM_STATE, di), F32)),
        grid=(nc,),
        in_specs=in_specs,
        out_specs=(wide(0), st),
        scratch_shapes=[pltpu.VMEM((SSM_STATE, di), F32)],
        compiler_params=_cp("arbitrary"),
        name="ssd_bw" if reverse else "ssd_fw",
    )(*args)


def _pad_cols(w, n):
    return jnp.pad(w, ((0, 0), (0, n - w.shape[1])))


def swa_mixer(h_ctx, h_lat, w_in, sink, cos2, sin2, with_ctx):
    w = w_in.astype(BF16)
    qkv_l = matmul(h_lat, w, F32, tn=1280)
    qkv_c = matmul(h_ctx, w, F32, tn=1280)
    o_lat, o_ctx = swa_attention(qkv_l, qkv_c, sink.astype(F32), cos2, sin2, with_ctx)
    return o_ctx, o_lat


def hgrn2_mixer(h_ctx, h_lat, w_in, lower_bound, norm_w, with_ctx):
    w = w_in.astype(BF16)
    hk = HG_HEADS * HG_KDIM
    lb = lower_bound.reshape(1, hk)
    nw = norm_w.reshape(1, hk)
    proj_c = matmul(h_ctx, w, F32, tn=1024)
    proj_l = matmul(h_lat, w, F32, tn=1024)
    zeros = jnp.zeros((HG_HEADS, HG_KDIM, HG_KDIM), F32)
    ofw_c, s_fw = hgrn2_scan(proj_c, lb, zeros, False)
    y_c, s_bw = hgrn2_scan(proj_c, lb, zeros, True, ofw_c, nw)
    ofw_l, _ = hgrn2_scan(proj_l, lb, s_fw, False)
    y_l, _ = hgrn2_scan(proj_l, lb, s_bw, True, ofw_l, nw)
    return (y_c if with_ctx else None), y_l


def mla_mixer(h_ctx, h_lat, w_in, q_norm, kv_norm, w_uq, w_ukv, cos2, sin2, with_ctx):
    t, nctx = h_lat.shape[0], h_ctx.shape[0]
    hh = MLA_HEADS
    ncol = MLA_Q_RANK + MLA_KV_RANK + LANES
    w1 = _pad_cols(w_in, ncol).astype(BF16)
    wq = w_uq.reshape(MLA_Q_RANK, hh, MLA_NOPE + MLA_ROPE)
    wq = jnp.pad(wq, ((0, 0), (0, 0), (0, LANES - MLA_ROPE))).reshape(MLA_Q_RANK, hh * (MLA_NOPE + LANES)).astype(BF16)
    wkv = w_ukv.reshape(MLA_KV_RANK, hh, MLA_NOPE + MLA_V)
    wkv = jnp.concatenate([wkv[:, :, :MLA_NOPE].reshape(MLA_KV_RANK, hh * MLA_NOPE),
                           wkv[:, :, MLA_NOPE:].reshape(MLA_KV_RANK, hh * MLA_V)], 1).astype(BF16)
    zpad = jnp.zeros((t, LANES - 2 * 32), F32)
    cos = jnp.concatenate([jnp.concatenate([cos2, zpad], 1), jnp.ones((nctx, LANES), F32)], 0)
    sin = jnp.concatenate([jnp.concatenate([sin2, zpad], 1), jnp.zeros((nctx, LANES), F32)], 0)
    h = jnp.concatenate([h_lat, h_ctx], 0)
    tm = 1280
    cq, ckv, kpe = mla_norm(matmul(h, w1, F32, tn=ncol, tm=tm), q_norm.reshape(1, -1), kv_norm.reshape(1, -1), cos, sin)
    q = matmul(cq, wq, F32, tn=1024, tm=tm)
    kv = matmul(ckv, wkv, BF16, tn=1024, tm=tm)
    o_ctx = mla_attention(q, cos, sin, kv, kpe, (t, nctx), (t, nctx), tq=nctx, tk=nctx) if with_ctx else None
    o_lat = mla_attention(q, cos, sin, kv, kpe, (0, t), (0, t + nctx), tq=1024, tk=1280)
    return o_ctx, o_lat


def mamba2_mixer(h_ctx, h_lat, w_in, conv_w, conv_b, dt_bias, a_log, d_skip, norm_w, with_ctx):
    di = norm_w.shape[0]
    gn = SSM_GROUPS * SSM_STATE
    heads = di // SSM_HEAD_DIM
    ncv = di + 2 * gn
    w_z, w_xbc, w_dt = (w_in[:, :di].astype(BF16), w_in[:, di:di + ncv].astype(BF16), w_in[:, di + ncv:].astype(BF16))
    dtb = dt_bias.reshape(1, 2 * heads).astype(F32)
    alog = a_log.reshape(1, 2 * heads).astype(F32)
    dsk = jnp.repeat(d_skip.astype(F32), SSM_HEAD_DIM).reshape(1, di)
    nw = norm_w.reshape(1, di)

    def prep(h):
        return (matmul_conv_silu(h, w_xbc, conv_w, conv_b), matmul(h, w_z, F32, tn=1024),
                matmul(h, w_dt, F32, tn=2 * heads))

    xbc_c, z_c, dt_c = prep(h_ctx)
    xbc_l, z_l, dt_l = prep(h_lat)
    zeros = jnp.zeros((SSM_STATE, di), F32)
    yfw_c, s_fw = ssd_scan(xbc_c, z_c, dt_c, dtb, alog, zeros, False)
    y_c, s_bw = ssd_scan(xbc_c, z_c, dt_c, dtb, alog, zeros, True, yfw_c, dsk, nw)
    yfw_l, _ = ssd_scan(xbc_l, z_l, dt_l, dtb, alog, s_fw, False)
    y_l, _ = ssd_scan(xbc_l, z_l, dt_l, dtb, alog, s_bw, True, yfw_l, dsk, nw)
    return (y_c if with_ctx else None), y_l


def kernel(x, c, ctx, c_ctx, ada_w, ada_b, ln_g, ln_b, ffn_up, ffn_conv_w, ffn_conv_b, ffn_down, swa_in, swa_sink, swa_out, hg_in, hg_lb_logits, hg_norm_w, hg_out, mla_in, mla_q_norm, mla_kv_norm, mla_uq, mla_ukv, mla_out, ssm_in, ssm_conv_w, ssm_conv_b, ssm_dt_bias, ssm_A_log, ssm_D, ssm_norm_w, ssm_out):
    batch, t, d = x.shape
    assert batch == 1
    depth = ada_w.shape[0]
    x_lat, x_ctx = x[0], ctx[0]

    probs = jax.nn.softmax(hg_lb_logits.astype(F32), axis=0)
    lower_bounds = jnp.cumsum(probs, axis=0) - probs[0]

    cvec = jnp.zeros((8, d), F32).at[0].set(c[0]).at[1].set(c_ctx)
    mods = ada_modulation(cvec, ada_w, ada_b)

    def mod(i, stream, j):
        return mods[i, stream:stream + 1, j * d:(j + 1) * d]

    cos2, sin2 = rope_tables(t)
    w_up, w_down = ffn_up.astype(BF16), ffn_down.astype(BF16)
    zero = jnp.zeros((1, d), F32)
    h_lat = modulate(x_lat, mod(0, 0, 0), mod(0, 0, 1))
    h_ctx = modulate(x_ctx, mod(0, 1, 0), mod(0, 1, 1))
    for i in range(depth):
        kind, j = i % N_MIXERS, i // N_MIXERS
        with_ctx = i < depth - 1
        if kind == 0:
            y_ctx, y_lat = swa_mixer(h_ctx, h_lat, swa_in[j], swa_sink[j], cos2, sin2, with_ctx)
            w_out = swa_out[j]
        elif kind == 1:
            y_ctx, y_lat = hgrn2_mixer(h_ctx, h_lat, hg_in[j], lower_bounds[i], hg_norm_w[j], with_ctx)
            w_out = hg_out[j]
        elif kind == 2:
            y_ctx, y_lat = mla_mixer(h_ctx, h_lat, mla_in[j], mla_q_norm[j], mla_kv_norm[j], mla_uq[j], mla_ukv[j],
                                     cos2, sin2, with_ctx)
            w_out = mla_out[j]
        else:
            y_ctx, y_lat = mamba2_mixer(h_ctx, h_lat, ssm_in[j], ssm_conv_w[j], ssm_conv_b[j], ssm_dt_bias[j],
                                        ssm_A_log[j], ssm_D[j], ssm_norm_w[j], with_ctx)
            w_out = ssm_out[j]
        w_out = w_out.astype(BF16)
        g0, b0, g1, b1 = ln_g[i, 0:1], ln_b[i, 0:1], ln_g[i, 1:2], ln_b[i, 1:2]
        streams = [(0, x_lat, y_lat)] + ([(1, x_ctx, y_ctx)] if with_ctx else [])
        outs = []
        for s, xs, ys in streams:
            xs, hf = matmul_ln(ys, w_out, xs, mod(i, s, 2), g0, b0, mod(i, s, 3), mod(i, s, 4))
            nsh, nsc = (mod(i + 1, s, 0), mod(i + 1, s, 1)) if i + 1 < depth else (zero, zero)
            outs.append(conv_ffn_ln(hf, w_up, ffn_conv_w[i], ffn_conv_b[i], w_down, i, xs, mod(i, s, 5), g1, b1, nsh, nsc))
        x_lat, h_lat = outs[0]
        if with_ctx:
            x_ctx, h_ctx = outs[1]
    return x_lat[None]
```
